```python
import math
import jax, jax.numpy as jnp
from jax import lax
import numpy as np


D_MODEL = 2048
BATCH = 2
SEQ = 16384
DEPTH = 4
DEC_BATCH = 8
DEC_SEQ = 4096
PAST_LEN = 128

D_MIX = D_MODEL
N_GROUPS = 4
GROUP_W = D_MIX // N_GROUPS
HEAD_DIM = 64
CONV_W = 4
CONV_PAD = (2, 1)
EPS = 1e-6
GDN_HEADS = GROUP_W // HEAD_DIM
GDN_CHUNK = 64
SWA_HEADS = GROUP_W // HEAD_DIM
SWA_KV_HEADS = 2
SWA_WINDOW = 128
SWA_BLOCK = 128
LRU_WIDTH = GROUP_W
LRU_BLOCKS = 8
LRU_BLOCK_W = LRU_WIDTH // LRU_BLOCKS
LRU_C = 8.0
DIFF_HEADS = GROUP_W // (2 * HEAD_DIM)
DIFF_QBLOCK = 128
D_FF = 4 * D_MODEL
IN_SIZES = (GROUP_W, GROUP_W, GROUP_W, GROUP_W, 2 * GDN_HEADS, 2 * GDN_HEADS,
            SWA_HEADS * HEAD_DIM, SWA_KV_HEADS * HEAD_DIM, SWA_KV_HEADS * HEAD_DIM,
            LRU_WIDTH, LRU_WIDTH,
            DIFF_HEADS * 2 * HEAD_DIM, DIFF_HEADS * 2 * HEAD_DIM, DIFF_HEADS * 2 * HEAD_DIM)
D_IN = 4 * GROUP_W + 4 * GDN_HEADS + (SWA_HEADS + 2 * SWA_KV_HEADS) * HEAD_DIM + 2 * LRU_WIDTH + 6 * DIFF_HEADS * HEAD_DIM

kernel_name = 'hybrid_bidir_parallel_group_encoder'


def rmsnorm(x, g):
    xf = x.astype(jnp.float32)
    y = xf * lax.rsqrt(jnp.mean(xf * xf, axis=-1, keepdims=True) + EPS)
    return (y * g.astype(jnp.float32)).astype(x.dtype)


def l2norm(x):
    xf = x.astype(jnp.float32)
    return xf * lax.rsqrt(jnp.sum(xf * xf, axis=-1, keepdims=True) + EPS)


def centred_dwconv(x, w):
    return lax.conv_general_dilated(
        x, w[:, None, :].astype(x.dtype), window_strides=(1,), padding=[CONV_PAD],
        dimension_numbers=('NWC', 'WIO', 'NWC'), feature_group_count=x.shape[-1])


def alibi_slopes(n):
    return 2.0 ** (-8.0 * jnp.arange(1, n + 1, dtype=jnp.float32) / n)


def gated_delta_chunked(q, k, v, g, beta):
    B, S, H, dk = q.shape
    dv = v.shape[-1]
    C = GDN_CHUNK
    N = S // C
    f32 = jnp.float32

    def chunks(t):
        return t.astype(f32).reshape((B, N, C, H) + t.shape[3:]).swapaxes(2, 3)

    qc = chunks(q) * (dk ** -0.5)
    kc, vc, bc = chunks(k), chunks(v), chunks(beta)
    gc = jnp.cumsum(chunks(g), axis=-1)
    tri = jnp.tril(jnp.ones((C, C), dtype=bool))
    strict = jnp.tril(jnp.ones((C, C), dtype=bool), -1)
    decay = jnp.exp(jnp.where(tri, gc[..., :, None] - gc[..., None, :], -jnp.inf))
    kb = kc * bc[..., None]
    a_mat = jnp.where(strict, jnp.einsum('bnhid,bnhjd->bnhij', kb, kc) * decay, 0.0)
    rhs = jnp.concatenate([vc * bc[..., None], kb * jnp.exp(gc)[..., None]], axis=-1)
    sol = lax.linalg.triangular_solve(a_mat, rhs, left_side=True, lower=True, unit_diagonal=True)
    u, w = sol[..., :dv], sol[..., dv:]
    intra = jnp.where(tri, jnp.einsum('bnhid,bnhjd->bnhij', qc, kc) * decay, 0.0)
    g_last = gc[..., -1]
    k_tail = kc * jnp.exp(g_last[..., None] - gc)[..., None]
    q_head = qc * jnp.exp(gc)[..., None]

    def step(state, xs):
        u_i, w_i, intra_i, kt_i, qh_i, gl_i = xs
        v_new = u_i - jnp.einsum('bhcd,bhde->bhce', w_i, state)
        o_i = jnp.einsum('bhcd,bhde->bhce', qh_i, state) + jnp.einsum('bhij,bhje->bhie', intra_i, v_new)
        state = state * jnp.exp(gl_i)[..., None, None] + jnp.einsum('bhcd,bhce->bhde', kt_i, v_new)
        return state, o_i

    xs = tuple(jnp.moveaxis(t, 1, 0) for t in (u, w, intra, k_tail, q_head, g_last))
    state0 = jnp.zeros((B, H, dk, dv), f32)
    _, o = lax.scan(step, state0, xs)
    return o.transpose(1, 0, 3, 2, 4).reshape(B, S, H, dv)


def gdn_mixer(q, k, v, z, a, b, conv_w, a_log, dt_bias, norm_g):
    B, S, _ = q.shape
    H, d = GDN_HEADS, HEAD_DIM
    f32 = jnp.float32
    qkv = jax.nn.silu(centred_dwconv(jnp.concatenate([q, k, v], axis=-1), conv_w)).reshape(B, S, 3, H, d)
    qh, kh, vh = l2norm(qkv[:, :, 0]), l2norm(qkv[:, :, 1]), qkv[:, :, 2].astype(f32)
    g = -jnp.exp(a_log.astype(f32)) * jax.nn.softplus(a.reshape(B, S, 2, H).astype(f32) + dt_bias.astype(f32))
    beta = jax.nn.sigmoid(b.reshape(B, S, 2, H).astype(f32))
    o_f = gated_delta_chunked(qh, kh, vh, g[:, :, 0], beta[:, :, 0])
    rev = lambda t: jnp.flip(t, axis=1)
    o_b = rev(gated_delta_chunked(rev(qh), rev(kh), rev(vh), rev(g[:, :, 1]), rev(beta[:, :, 1])))
    o = rmsnorm(o_f + o_b, norm_g) * jax.nn.silu(z.reshape(B, S, H, d).astype(f32))
    return o.reshape(B, S, H * d).astype(z.dtype)


def swa_mixer(q, k, v, sink):
    B, S, _ = q.shape
    T, KV, d = SWA_BLOCK, SWA_KV_HEADS, HEAD_DIM
    G = SWA_HEADS // KV
    NB = S // T
    qb = q.reshape(B, NB, T, KV, G, d)

    def band(t):
        tp = jnp.pad(t.reshape(B, S, KV, d), ((0, 0), (T, T), (0, 0), (0, 0))).reshape(B, NB + 2, T, KV, d)
        return jnp.concatenate([tp[:, :-2], tp[:, 1:-1], tp[:, 2:]], axis=2)

    kb, vb = band(k), band(v)
    s = jnp.einsum('bnqkgd,bnskd->bnkgqs', qb, kb).astype(jnp.float32) * (d ** -0.5)
    qpos = jnp.arange(NB)[:, None] * T + jnp.arange(T)[None, :]
    kpos = jnp.arange(NB)[:, None] * T - T + jnp.arange(3 * T)[None, :]
    dist = jnp.abs(qpos[:, :, None] - kpos[:, None, :])
    valid = (dist <= SWA_WINDOW) & (kpos >= 0)[:, None, :] & (kpos < S)[:, None, :]
    slopes = alibi_slopes(SWA_HEADS).reshape(KV, G)
    s = s - slopes[:, :, None, None] * dist[:, None, None].astype(jnp.float32)
    s = jnp.where(valid[:, None, None], s, -jnp.inf)
    sk = sink.astype(jnp.float32).reshape(KV, G)[:, :, None, None]
    m = jnp.maximum(jnp.max(s, axis=-1, keepdims=True), sk)
    p = jnp.exp(s - m)
    p = p / (jnp.sum(p, axis=-1, keepdims=True) + jnp.exp(sk - m))
    o = jnp.einsum('bnkgqs,bnskd->bnqkgd', p.astype(v.dtype), vb)
    return o.reshape(B, S, SWA_HEADS * d)


def lru_mixer(xb, gate, conv_w, conv_b, w_gate, b_gate, lam):
    B, S, W = xb.shape
    f32 = jnp.float32
    xc = centred_dwconv(xb, conv_w) + conv_b.astype(xb.dtype)
    gates = jnp.einsum('bsnd,rgnde->rgbsne', xc.reshape(B, S, LRU_BLOCKS, LRU_BLOCK_W), w_gate.astype(xb.dtype))
    gates = jax.nn.sigmoid(gates.reshape(2, 2, B, S, W).astype(f32) + b_gate[:, :, None, None, :].astype(f32))
    r, i = gates[:, 0], gates[:, 1]
    log_a = -LRU_C * r * jax.nn.softplus(-lam.astype(f32))[:, None, None, :]
    a = jnp.exp(log_a)
    u = jnp.sqrt(-jnp.expm1(2.0 * log_a)) * i * xc.astype(f32)[None]

    def combine(left, right):
        a1, b1 = left
        a2, b2 = right
        return a1 * a2, a2 * b1 + b2

    h_f = lax.associative_scan(combine, (a[0], u[0]), axis=1)[1]
    h_b = lax.associative_scan(combine, (a[1], u[1]), axis=1, reverse=True)[1]
    return ((h_f + h_b) * jax.nn.gelu(gate.astype(f32))).astype(xb.dtype)


def diff_mixer(q, k, v, lam_vec, norm_g, lam_init):
    B, S, _ = q.shape
    H, d, QB = DIFF_HEADS, HEAD_DIM, DIFF_QBLOCK
    NQ = S // QB
    k = k.reshape(B, S, H, 2, d)
    v = v.reshape(B, S, H, 2 * d)
    qb = jnp.moveaxis(q.reshape(B, NQ, QB, H, 2, d), 1, 0)
    lv = lam_vec.astype(jnp.float32)
    lam = jnp.exp(jnp.sum(lv[0] * lv[1])) - jnp.exp(jnp.sum(lv[2] * lv[3])) + lam_init
    slopes = alibi_slopes(H)[:, None, None, None]
    kpos = jnp.arange(S)

    def block(args):
        q_i, n = args
        s = jnp.einsum('bqhmd,bshmd->bhmqs', q_i, k).astype(jnp.float32) * (d ** -0.5)
        dist = jnp.abs(n * QB + jnp.arange(QB)[:, None] - kpos[None, :]).astype(jnp.float32)
        p = jax.nn.softmax(s - slopes * dist, axis=-1)
        wts = p[:, :, 0] - lam * p[:, :, 1]
        return jnp.einsum('bhqs,bshe->bqhe', wts.astype(v.dtype), v)

    o = lax.map(block, (qb, jnp.arange(NQ)))
    o = jnp.moveaxis(o, 0, 1).reshape(B, S, H, 2 * d)
    o = rmsnorm(o, norm_g) * (1.0 - lam_init)
    return o.reshape(B, S, H * 2 * d)


def trunk(x, c, weights):
    (norm1_g, norm2_g, w_mod, b_mod, w_in, w_out, gdn_conv, gdn_a_log, gdn_dt_bias, gdn_norm_g,
     swa_sink, lru_conv, lru_conv_b, lru_w_gate, lru_b_gate, lru_lambda, diff_lambda, diff_norm_g,
     mlp_w1, mlp_w2, final_g) = weights
    split_idx = np.cumsum(IN_SIZES)[:-1].tolist()
    for l in range(DEPTH):
        mod = jax.nn.silu(c) @ w_mod[l] + b_mod[l]
        sh1, sc1, g1, sh2, sc2, g2 = jnp.split(mod[:, None, :], 6, axis=-1)
        h = rmsnorm(x, norm1_g[l]) * (1 + sc1) + sh1
        (a_q, a_k, a_v, a_z, a_a, a_b, b_q, b_k, b_v, c_x, c_g, d_q, d_k, d_v) = jnp.split(h @ w_in[l], split_idx, axis=-1)
        o_a = gdn_mixer(a_q, a_k, a_v, a_z, a_a, a_b, gdn_conv[l], gdn_a_log[l], gdn_dt_bias[l], gdn_norm_g[l])
        o_b = swa_mixer(b_q, b_k, b_v, swa_sink[l])
        o_c = lru_mixer(c_x, c_g, lru_conv[l], lru_conv_b[l], lru_w_gate[l], lru_b_gate[l], lru_lambda[l])
        lam_init = 0.8 - 0.6 * math.exp(-0.3 * l)
        o_d = diff_mixer(d_q, d_k, d_v, diff_lambda[l], diff_norm_g[l], lam_init)
        x = x + g1 * (jnp.concatenate([o_a, o_b, o_c, o_d], axis=-1) @ w_out[l])
        h = rmsnorm(x, norm2_g[l]) * (1 + sc2) + sh2
        x = x + g2 * (jnp.square(jax.nn.relu(h @ mlp_w1[l])) @ mlp_w2[l])
    return rmsnorm(x, final_g)


def setup_inputs(seed: int = 0) -> dict:
    key = jax.random.key(seed)
    ks = jax.random.split(key, 32)
    f32 = jnp.float32

    def nrm(k, shape, std):
        return std * jax.random.normal(k, shape, f32)

    dt = jnp.exp(jax.random.uniform(ks[12], (DEPTH, 2, GDN_HEADS), f32, math.log(1e-3), math.log(1e-1)))
    a0 = jax.random.uniform(ks[19], (DEPTH, 2, LRU_WIDTH), f32, 0.9, 0.999)
    return {
        'x_prompt': nrm(ks[0], (BATCH, SEQ, D_MODEL), 1.0),
        'x_sample': nrm(ks[1], (DEC_BATCH, DEC_SEQ, D_MODEL), 1.0),
        'c_prompt': nrm(ks[2], (BATCH, D_MODEL), 1.0),
        'c_sample': nrm(ks[3], (DEC_BATCH, D_MODEL), 1.0),
        'norm1_g': 1.0 + nrm(ks[4], (DEPTH, D_MODEL), 0.02),
        'norm2_g': 1.0 + nrm(ks[5], (DEPTH, D_MODEL), 0.02),
        'w_mod': nrm(ks[6], (DEPTH, D_MODEL, 6 * D_MODEL), 0.5 * D_MODEL ** -0.5),
        'b_mod': nrm(ks[7], (DEPTH, 6 * D_MODEL), 0.02),
        'w_in': nrm(ks[8], (DEPTH, D_MODEL, D_IN), D_MODEL ** -0.5),
        'w_out': nrm(ks[9], (DEPTH, D_MIX, D_MODEL), D_MIX ** -0.5),
        'gdn_conv': nrm(ks[10], (DEPTH, CONV_W, 3 * GROUP_W), CONV_W ** -0.5),
        'gdn_a_log': jnp.log(jax.random.uniform(ks[11], (DEPTH, 2, GDN_HEADS), f32, 1.0, 16.0)),
        'gdn_dt_bias': dt + jnp.log(-jnp.expm1(-dt)),
        'gdn_norm_g': 1.0 + nrm(ks[13], (DEPTH, HEAD_DIM), 0.02),
        'swa_sink': nrm(ks[14], (DEPTH, SWA_HEADS), 0.5),
        'lru_conv': nrm(ks[15], (DEPTH, CONV_W, LRU_WIDTH), CONV_W ** -0.5),
        'lru_conv_b': nrm(ks[16], (DEPTH, LRU_WIDTH), 0.01),
        'lru_w_gate': nrm(ks[17], (DEPTH, 2, 2, LRU_BLOCKS, LRU_BLOCK_W, LRU_BLOCK_W), LRU_BLOCK_W ** -0.5),
        'lru_b_gate': nrm(ks[18], (DEPTH, 2, 2, LRU_WIDTH), 0.01),
        'lru_lambda': jnp.log(a0) - jnp.log1p(-a0),
        'diff_lambda': nrm(ks[20], (DEPTH, 4, HEAD_DIM), 0.1),
        'diff_norm_g': 1.0 + nrm(ks[21], (DEPTH, 2 * HEAD_DIM), 0.02),
        'mlp_w1': nrm(ks[22], (DEPTH, D_MODEL, D_FF), D_MODEL ** -0.5),
        'mlp_w2': nrm(ks[23], (DEPTH, D_FF, D_MODEL), D_FF ** -0.5),
        'final_g': 1.0 + nrm(ks[24], (D_MODEL,), 0.02),
    }


def reference(x_prompt, x_sample, c_prompt, c_sample, norm1_g, norm2_g, w_mod, b_mod, w_in, w_out,
              gdn_conv, gdn_a_log, gdn_dt_bias, gdn_norm_g, swa_sink, lru_conv, lru_conv_b,
              lru_w_gate, lru_b_gate, lru_lambda, diff_lambda, diff_norm_g, mlp_w1, mlp_w2, final_g):
    weights = (norm1_g, norm2_g, w_mod, b_mod, w_in, w_out, gdn_conv, gdn_a_log, gdn_dt_bias, gdn_norm_g,
               swa_sink, lru_conv, lru_conv_b, lru_w_gate, lru_b_gate, lru_lambda, diff_lambda, diff_norm_g,
               mlp_w1, mlp_w2, final_g)
    y_prompt = trunk(x_prompt, c_prompt, weights)
    y_sample = trunk(x_sample, c_sample, weights)
    return (y_prompt, y_sample)
```

```python
import functools
import math

import numpy as np
import jax
import jax.numpy as jnp
from jax import lax
from jax.experimental import pallas as pl
from jax.experimental.pallas import tpu as pltpu

f32 = jnp.float32
bf16 = jnp.bfloat16

D_MODEL = 2048
GROUP_W = 512
HEAD_DIM = 64
EPS = 1e-6
GDN_HEADS = 8
GDN_CHUNK = 64
SWA_HEADS = 8
SWA_BLOCK = 128
LRU_C = 8.0
DIFF_HEADS = 4
D_FF = 4 * D_MODEL
N_MOD = 6 * D_MODEL
MOD_ROWS = 16

OFF_AQKV, OFF_AZ, OFF_BQ, OFF_CX, OFF_CG = 0, 1536, 2048, 2560, 3072
OFF_DQ, OFF_DK, OFF_DV, OFF_BK, OFF_BV, OFF_AAB = 3584, 4096, 4608, 5120, 5376, 5632
N_PROJ = 5760
PROJ_TN = 1152
NEG = -1e30

VMEM_LIMIT = 56 * 1024 * 1024


def _proj_columns():
    src = np.full((N_PROJ,), -1, np.int64)
    src[0:2048] = np.arange(0, 2048)
    src[OFF_BQ:OFF_BQ + 512] = np.arange(2080, 2592)
    src[OFF_CX:OFF_CX + 512] = np.arange(2848, 3360)
    src[OFF_CG:OFF_CG + 512] = np.arange(3360, 3872)
    src[OFF_DQ:OFF_DQ + 512] = np.arange(3872, 4384)
    src[OFF_DK:OFF_DK + 512] = np.arange(4384, 4896)
    src[OFF_DV:OFF_DV + 512] = np.arange(4896, 5408)
    for kv in range(2):
        for half in range(2):
            o = 128 * kv + 64 * half
            src[OFF_BK + o:OFF_BK + o + 64] = np.arange(2592 + 64 * kv, 2592 + 64 * kv + 64)
            src[OFF_BV + o:OFF_BV + o + 64] = np.arange(2720 + 64 * kv, 2720 + 64 * kv + 64)
    src[OFF_AAB:OFF_AAB + 32] = np.arange(2048, 2080)
    return src


def _cparams(sem):
    return pltpu.CompilerParams(dimension_semantics=sem, vmem_limit_bytes=VMEM_LIMIT)


def _dot(a, b):
    return jnp.dot(a, b, preferred_element_type=f32)


def _dot_nt(a, b):
    return lax.dot_general(a, b, (((1,), (1,)), ((), ())), preferred_element_type=f32)


def _dot_tn(a, b):
    return lax.dot_general(a, b, (((0,), (0,)), ((), ())), preferred_element_type=f32)


def _split3(x):
    hi = x.astype(bf16)
    r = x - hi.astype(f32)
    mid = r.astype(bf16)
    lo = (r - mid.astype(f32)).astype(bf16)
    return hi, mid, lo


def _dot_sel(x, sel):
    hi, mid, lo = _split3(x)
    return _dot(hi, sel) + _dot(mid, sel) + _dot(lo, sel)


def _sel_dot(sel, x):
    hi, mid, lo = _split3(x)
    return _dot(sel, hi) + _dot(sel, mid) + _dot(sel, lo)


def _dot_hi(x, y):
    xh = x.astype(bf16)
    xl = (x - xh.astype(f32)).astype(bf16)
    yh = y.astype(bf16)
    yl = (y - yh.astype(f32)).astype(bf16)
    return _dot(xh, yh) + _dot(xl, yh) + _dot(xh, yl)


def _sigmoid(x):
    return 1.0 / (1.0 + jnp.exp(-x))


def _softplus(x):
    return jnp.maximum(x, 0.0) + jnp.log(1.0 + jnp.exp(-jnp.abs(x)))


def _silu(x):
    return x * _sigmoid(x)


def _gelu_tanh(x):
    return 0.5 * x * (1.0 + jnp.tanh(math.sqrt(2.0 / math.pi) * (x + 0.044715 * (x * x * x))))


def _norm_mod(x, g, sc, sh):
    var = jnp.mean(x * x, axis=-1, keepdims=True)
    return (x * lax.rsqrt(var + EPS) * g) * (1.0 + sc) + sh


def _conv4(x, prev, nxt, w):
    T = x.shape[0]
    row = lax.broadcasted_iota(jnp.int32, x.shape, 0)
    xm2 = pltpu.roll(x, 2, 0)
    xm2 = jnp.where(row == 0, prev[6:7, :], jnp.where(row == 1, prev[7:8, :], xm2))
    xm1 = jnp.where(row == 0, prev[7:8, :], pltpu.roll(x, 1, 0))
    xp1 = jnp.where(row == T - 1, nxt[0:1, :], pltpu.roll(x, T - 1, 0))
    return w[0:1, :] * xm2 + w[1:2, :] * xm1 + w[2:3, :] * x + w[3:4, :] * xp1


def _mod_kernel(c_ref, w_ref, b_ref, o_ref):
    c = c_ref[...]
    o_ref[0] = _dot(_silu(c).astype(bf16), w_ref[0].astype(bf16)) + b_ref[0]


def _modulation(c_all, w_mod, b_mod):
    depth = w_mod.shape[0]
    tn = 1024
    return pl.pallas_call(
        _mod_kernel,
        grid=(depth, N_MOD // tn),
        in_specs=[
            pl.BlockSpec((MOD_ROWS, D_MODEL), lambda l, j: (0, 0)),
            pl.BlockSpec((1, D_MODEL, tn), lambda l, j: (l, 0, j)),
            pl.BlockSpec((1, 1, tn), lambda l, j: (l, 0, j)),
        ],
        out_specs=pl.BlockSpec((1, MOD_ROWS, tn), lambda l, j: (l, 0, j)),
        out_shape=jax.ShapeDtypeStruct((depth, MOD_ROWS, N_MOD), f32),
        compiler_params=_cparams(("arbitrary", "arbitrary")),
        name="modulation",
    )(c_all, w_mod, b_mod.reshape(depth, 1, N_MOD))


def _mod_spec(row0, chunk, ngrid):
    def imap(*a):
        b, l_ref = a[0], a[ngrid]
        return (l_ref[0] * (MOD_ROWS * 6) + (row0 + b) * 6 + chunk, 0, 0)
    return pl.BlockSpec((None, 1, D_MODEL), imap)


def _norm_proj_kernel(l_ref, x_ref, g_ref, sc_ref, sh_ref, w_ref, o_ref, h_ref):
    @pl.when(pl.program_id(2) == 0)
    def _():
        h_ref[...] = _norm_mod(x_ref[...], g_ref[...], sc_ref[...], sh_ref[...]).astype(bf16)

    o_ref[...] = _dot(h_ref[...], w_ref[...]).astype(o_ref.dtype)


def _norm_proj(l, x, norm_g, mod, row0, w_p):
    B, S, _ = x.shape
    tm = min(1024, S)
    grid_spec = pltpu.PrefetchScalarGridSpec(
        num_scalar_prefetch=1,
        grid=(B, S // tm, N_PROJ // PROJ_TN),
        in_specs=[
            pl.BlockSpec((None, tm, D_MODEL), lambda b, i, j, l_ref: (b, i, 0)),
            pl.BlockSpec((None, 1, D_MODEL), lambda b, i, j, l_ref: (l_ref[0], 0, 0)),
            _mod_spec(row0, 1, 3),
            _mod_spec(row0, 0, 3),
            pl.BlockSpec((None, D_MODEL, PROJ_TN), lambda b, i, j, l_ref: (l_ref[0], 0, j)),
        ],
        out_specs=pl.BlockSpec((None, tm, PROJ_TN), lambda b, i, j, l_ref: (b, i, j)),
        scratch_shapes=[pltpu.VMEM((tm, D_MODEL), bf16)],
    )
    return pl.pallas_call(
        _norm_proj_kernel,
        grid_spec=grid_spec,
        out_shape=jax.ShapeDtypeStruct((B, S, N_PROJ), f32),
        compiler_params=_cparams(("arbitrary", "arbitrary", "arbitrary")),
        name="norm_proj",
    )(l, x, norm_g, mod, mod, w_p)


def _gdn_kernel(l_ref, x_ref, xp_ref, xn_ref, ab_ref, *rest, rev, final):
    if final:
        (z_ref, ob_ref, cw_ref, alog_ref, dtb_ref, ng_ref, ehead_ref, expg_ref, expb_ref,
         o_ref, s_ref) = rest
    else:
        (cw_ref, alog_ref, dtb_ref, ehead_ref, expg_ref, expb_ref, o_ref, s_ref) = rest
    C = GDN_CHUNK
    i = pl.program_id(1)
    nb = pl.num_programs(1)
    blk = nb - 1 - i if rev else i

    @pl.when(i == 0)
    def _():
        s_ref[...] = jnp.zeros_like(s_ref)

    prev = jnp.where(blk > 0, xp_ref[...], 0.0)
    nxt = jnp.where(blk < nb - 1, xn_ref[...], 0.0)
    y = _silu(_conv4(x_ref[...], prev, nxt, cw_ref[...]))
    q, k, v = y[:, 0:512], y[:, 512:1024], y[:, 1024:1536]
    ehead = ehead_ref[...]
    qn = q * lax.rsqrt(_dot_sel(q * q, ehead) + EPS) * (HEAD_DIM ** -0.5)
    kn = k * lax.rsqrt(_dot_sel(k * k, ehead) + EPS)

    gin = ab_ref[...]
    g = -jnp.exp(alog_ref[...]) * _softplus(gin + dtb_ref[...])
    ri = lax.broadcasted_iota(jnp.int32, (C, C), 0)
    ci = lax.broadcasted_iota(jnp.int32, (C, C), 1)
    cum = jnp.where((ci >= ri) if rev else (ci <= ri), 1.0, 0.0).astype(bf16)
    ones_cc = jnp.ones((C, C), bf16)
    gc = _sel_dot(cum, g)
    gl = _sel_dot(ones_cc, g)
    expg = expg_ref[...]
    gc_x = _dot_sel(gc, expg)
    gl_x = _dot_sel(gl, expg)
    beta_x = _sigmoid(_dot_sel(gin, expb_ref[...]))
    egc_x = jnp.exp(gc_x)
    etail_x = jnp.exp(gl_x - gc_x)
    egl_x = jnp.exp(gl_x)

    kb = kn * beta_x
    vb = v * beta_x
    kw = kb * egc_x
    qh = qn * egc_x
    kt = kn * etail_x

    W = 4 * C
    r2 = lax.broadcasted_iota(jnp.int32, (W, W), 0) // C
    c2 = lax.broadcasted_iota(jnp.int32, (W, W), 1) // C
    bd = r2 == c2
    rs = lax.broadcasted_iota(jnp.int32, (C, W), 0)
    js = lax.broadcasted_iota(jnp.int32, (C, W), 1) % C
    tri = (js >= rs) if rev else (js <= rs)
    strict = (js > rs) if rev else (js < rs)
    eye_s = jnp.where(js == rs, 1.0, 0.0)

    def blockdiag(m):
        return jnp.where(bd, jnp.concatenate([m, m, m, m], axis=0), 0.0)

    outs = []
    for grp in range(2):
        sl = slice(W * grp, W * grp + W)
        kst = blockdiag(kn[:, sl]).astype(bf16)
        kk = _dot_nt(kb[:, sl].astype(bf16), kst)
        qk = _dot_nt(qn[:, sl].astype(bf16), kst)
        col = gc_x[:, sl]
        rowv = _sel_dot(ones_cc, col * eye_s)
        dm = jnp.exp(jnp.where(tri, col - rowv, NEG))
        a = jnp.where(strict, kk * dm, 0.0)
        intra = qk * dm
        p = a
        t = eye_s - a
        for _ in range(5):
            p = _dot_hi(p, blockdiag(p))
            t = t + _dot_hi(t, blockdiag(p))
        tb = t.astype(bf16)
        u = _dot(tb, blockdiag(vb[:, sl]).astype(bf16))
        w = _dot(tb, blockdiag(kw[:, sl]).astype(bf16))
        st = s_ref[grp]
        wq = jnp.concatenate([w, qh[:, sl]], axis=0).astype(bf16)
        ws = _dot(wq, st.astype(bf16))
        vnew = u - ws[0:C]
        o = ws[C:2 * C] + _dot(intra.astype(bf16), blockdiag(vnew).astype(bf16))
        upd = _dot_tn(kt[:, sl].astype(bf16), vnew.astype(bf16))
        s_ref[grp] = st * egl_x[0:1, sl] + jnp.where(bd, upd, 0.0)
        outs.append(o)
    o_full = jnp.concatenate(outs, axis=1)
    if final:
        tot = o_full + ob_ref[...]
        ms = _dot_sel(tot * tot, ehead) * (1.0 / HEAD_DIM)
        o_ref[...] = (tot * lax.rsqrt(ms + EPS) * ng_ref[...] * _silu(z_ref[...])).astype(o_ref.dtype)
    else:
        o_ref[...] = o_full


def _gdn_consts():
    lane = np.arange(512)
    ehead = (lane[:, None] // HEAD_DIM == lane[None, :] // HEAD_DIM).astype(np.float32)
    expg = np.zeros((2, 128, 512), np.float32)
    expb = np.zeros((2, 128, 512), np.float32)
    for r in range(2):
        expg[r, 8 * r + lane // HEAD_DIM, lane] = 1.0
        expb[r, 16 + 8 * r + lane // HEAD_DIM, lane] = 1.0
    return jnp.asarray(ehead, bf16), jnp.asarray(expg, bf16), jnp.asarray(expb, bf16)


def _gdn_pass(l, proj, conv_w, alog_row, dtb_row, ng_row, consts, rev, ob):
    B, S, _ = proj.shape
    C = GDN_CHUNK
    nb = S // C
    n8 = S // 8
    final = not rev
    ehead, expg, expb = consts

    def pos(i):
        return nb - 1 - i if rev else i

    in_specs = [
        pl.BlockSpec((None, C, 1536), lambda b, i, l_ref: (b, pos(i), 0)),
        pl.BlockSpec((None, 8, 1536), lambda b, i, l_ref: (b, jnp.maximum(pos(i) * (C // 8) - 1, 0), 0)),
        pl.BlockSpec((None, 8, 1536), lambda b, i, l_ref: (b, jnp.minimum((pos(i) + 1) * (C // 8), n8 - 1), 0)),
        pl.BlockSpec((None, C, 128), lambda b, i, l_ref: (b, pos(i), OFF_AAB // 128)),
    ]
    args = [proj, proj, proj, proj]
    if final:
        in_specs += [
            pl.BlockSpec((None, C, 512), lambda b, i, l_ref: (b, pos(i), OFF_AZ // 512)),
            pl.BlockSpec((None, C, 512), lambda b, i, l_ref: (b, pos(i), 0)),
        ]
        args += [proj, ob]
    in_specs += [
        pl.BlockSpec((None, 4, 1536), lambda b, i, l_ref: (l_ref[0], 0, 0)),
        pl.BlockSpec((None, 1, 128), lambda b, i, l_ref: (l_ref[0], 0, 0)),
        pl.BlockSpec((None, 1, 128), lambda b, i, l_ref: (l_ref[0], 0, 0)),
    ]
    args += [conv_w, alog_row, dtb_row]
    if final:
        in_specs.append(pl.BlockSpec((None, 1, 512), lambda b, i, l_ref: (l_ref[0], 0, 0)))
        args.append(ng_row)
    in_specs += [
        pl.BlockSpec((512, 512), lambda b, i, l_ref: (0, 0)),
        pl.BlockSpec((None, 128, 512), lambda b, i, l_ref: (1 if rev else 0, 0, 0)),
        pl.BlockSpec((None, 128, 512), lambda b, i, l_ref: (1 if rev else 0, 0, 0)),
    ]
    args += [ehead, expg, expb]
    grid_spec = pltpu.PrefetchScalarGridSpec(
        num_scalar_prefetch=1,
        grid=(B, nb),
        in_specs=in_specs,
        out_specs=pl.BlockSpec((None, C, 512), lambda b, i, l_ref: (b, pos(i), 0)),
        scratch_shapes=[pltpu.VMEM((2, 4 * C, 4 * C), f32)],
    )
    return pl.pallas_call(
        functools.partial(_gdn_kernel, rev=rev, final=final),
        grid_spec=grid_spec,
        out_shape=jax.ShapeDtypeStruct((B, S, 512), bf16 if final else f32),
        compiler_params=_cparams(("arbitrary", "arbitrary")),
        name="gdn_fwd" if final else "gdn_bwd",
    )(l, *args)


def _swa_kernel(l_ref, sink_ref, q_ref, kp_ref, kc_ref, kn_ref, vp_ref, vc_ref, vn_ref, o_ref):
    T = SWA_BLOCK
    i = pl.program_id(1)
    nb = pl.num_programs(1)
    l = l_ref[0]
    row = lax.broadcasted_iota(jnp.int32, (2 * T, 3 * T), 0)
    col = lax.broadcasted_iota(jnp.int32, (2 * T, 3 * T), 1)
    top = row < T
    r = jnp.where(top, row, row - T)
    disti = jnp.abs(r + T - col)
    dist = disti.astype(f32)
    valid = (disti <= SWA_BLOCK) & ((col >= T) | (i > 0)) & ((col < 2 * T) | (i < nb - 1))
    lo = lax.broadcasted_iota(jnp.int32, (T, 128), 1) < HEAD_DIM
    top1 = lax.broadcasted_iota(jnp.int32, (2 * T, 1), 0) < T
    for kv in range(2):
        ksl = slice(128 * kv, 128 * kv + 128)
        kd = jnp.concatenate([kp_ref[:, ksl], kc_ref[:, ksl], kn_ref[:, ksl]], axis=0).astype(bf16)
        vd = jnp.concatenate([vp_ref[:, ksl], vc_ref[:, ksl], vn_ref[:, ksl]], axis=0).astype(bf16)
        for pp in range(2):
            p = 2 * kv + pp
            qp = q_ref[:, 128 * p:128 * p + 128] * (HEAD_DIM ** -0.5)
            lhs = jnp.concatenate([jnp.where(lo, qp, 0.0), jnp.where(lo, 0.0, qp)], axis=0).astype(bf16)
            s = _dot_nt(lhs, kd)
            slope = jnp.where(top, 2.0 ** -(2 * p + 1), 2.0 ** -(2 * p + 2))
            s = jnp.where(valid, s - slope * dist, NEG)
            sk = jnp.where(top1, sink_ref[l, 2 * p], sink_ref[l, 2 * p + 1])
            m = jnp.maximum(jnp.max(s, axis=1, keepdims=True), sk)
            e = jnp.exp(s - m)
            den = jnp.sum(e, axis=1, keepdims=True) + jnp.exp(sk - m)
            o = _dot(e.astype(bf16), vd) / den
            o_ref[:, 128 * p:128 * p + 128] = jnp.where(lo, o[0:T], o[T:2 * T]).astype(o_ref.dtype)


def _swa(l, proj, sink):
    B, S, _ = proj.shape
    T = SWA_BLOCK
    nb = S // T
    kb, vb = OFF_BK // 256, OFF_BV // 256

    def side(blk, which):
        if which < 0:
            return lambda b, i, l_ref: (b, jnp.maximum(i - 1, 0), blk)
        if which > 0:
            return lambda b, i, l_ref: (b, jnp.minimum(i + 1, nb - 1), blk)
        return lambda b, i, l_ref: (b, i, blk)

    grid_spec = pltpu.PrefetchScalarGridSpec(
        num_scalar_prefetch=1,
        grid=(B, nb),
        in_specs=[
            pl.BlockSpec(memory_space=pltpu.SMEM),
            pl.BlockSpec((None, T, 512), lambda b, i, l_ref: (b, i, OFF_BQ // 512)),
            pl.BlockSpec((None, T, 256), side(kb, -1)),
            pl.BlockSpec((None, T, 256), side(kb, 0)),
            pl.BlockSpec((None, T, 256), side(kb, 1)),
            pl.BlockSpec((None, T, 256), side(vb, -1)),
            pl.BlockSpec((None, T, 256), side(vb, 0)),
            pl.BlockSpec((None, T, 256), side(vb, 1)),
        ],
        out_specs=pl.BlockSpec((None, T, 512), lambda b, i, l_ref: (b, i, 0)),
    )
    return pl.pallas_call(
        _swa_kernel,
        grid_spec=grid_spec,
        out_shape=jax.ShapeDtypeStruct((B, S, 512), bf16),
        compiler_params=_cparams(("arbitrary", "arbitrary")),
        name="swa",
    )(l, sink, proj, proj, proj, proj, proj, proj, proj)


def _lru_kernel(l_ref, x_ref, xp_ref, xn_ref, *rest, rev, final):
    if final:
        (gate_ref, hb_ref, cw_ref, cb_ref, wg_ref, bg_ref, lam_ref, o_ref, carry_ref) = rest
    else:
        (cw_ref, cb_ref, wg_ref, bg_ref, lam_ref, o_ref, carry_ref) = rest
    T = x_ref.shape[0]
    i = pl.program_id(1)
    nb = pl.num_programs(1)
    blk = nb - 1 - i if rev else i

    @pl.when(i == 0)
    def _():
        carry_ref[...] = jnp.zeros_like(carry_ref)

    prev = jnp.where(blk > 0, xp_ref[...], 0.0)
    nxt = jnp.where(blk < nb - 1, xn_ref[...], 0.0)
    xc = _conv4(x_ref[...], prev, nxt, cw_ref[...]) + cb_ref[...]
    gates = _sigmoid(_dot(xc.astype(bf16), wg_ref[...]) + bg_ref[...])
    rg, ig = gates[:, 0:512], gates[:, 512:1024]
    log_a = (-LRU_C) * rg * _softplus(-lam_ref[...])
    a = jnp.exp(log_a)
    y2 = 2.0 * log_a
    one_minus_a2 = -jnp.tanh(0.5 * y2) * (jnp.exp(y2) + 1.0)
    bv = jnp.sqrt(one_minus_a2) * ig * xc
    row = lax.broadcasted_iota(jnp.int32, (T, 512), 0)
    s = 1
    while s < T:
        if rev:
            ok = row < T - s
            a_s, b_s = pltpu.roll(a, T - s, 0), pltpu.roll(bv, T - s, 0)
        else:
            ok = row >= s
            a_s, b_s = pltpu.roll(a, s, 0), pltpu.roll(bv, s, 0)
        bv = jnp.where(ok, a * b_s + bv, bv)
        a = jnp.where(ok, a * a_s, a)
        s *= 2
    h = a * carry_ref[...] + bv
    carry_ref[...] = h[0:1, :] if rev else h[T - 1:T, :]
    if final:
        o_ref[...] = ((h + hb_ref[...]) * _gelu_tanh(gate_ref[...])).astype(o_ref.dtype)
    else:
        o_ref[...] = h


def _lru_pass(l, proj, conv_w, conv_b, wg, bg, lam, rev, hb):
    B, S, _ = proj.shape
    T = min(256, S)
    nb = S // T
    n8 = S // 8
    final = not rev
    d = 1 if rev else 0
    xb = OFF_CX // 512

    def pos(i):
        return nb - 1 - i if rev else i

    in_specs = [
        pl.BlockSpec((None, T, 512), lambda b, i, l_ref: (b, pos(i), xb)),
        pl.BlockSpec((None, 8, 512), lambda b, i, l_ref: (b, jnp.maximum(pos(i) * (T // 8) - 1, 0), xb)),
        pl.BlockSpec((None, 8, 512), lambda b, i, l_ref: (b, jnp.minimum((pos(i) + 1) * (T // 8), n8 - 1), xb)),
    ]
    args = [proj, proj, proj]
    if final:
        in_specs += [
            pl.BlockSpec((None, T, 512), lambda b, i, l_ref: (b, pos(i), OFF_CG // 512)),
            pl.BlockSpec((None, T, 512), lambda b, i, l_ref: (b, pos(i), 0)),
        ]
        args += [proj, hb]
    in_specs += [
        pl.BlockSpec((None, 4, 512), lambda b, i, l_ref: (l_ref[0], 0, 0)),
        pl.BlockSpec((None, 1, 512), lambda b, i, l_ref: (l_ref[0], 0, 0)),
        pl.BlockSpec((None, None, 512, 1024), lambda b, i, l_ref: (l_ref[0], d, 0, 0)),
        pl.BlockSpec((None, None, 1, 1024), lambda b, i, l_ref: (l_ref[0], d, 0, 0)),
        pl.BlockSpec((None, None, 1, 512), lambda b, i, l_ref: (l_ref[0], d, 0, 0)),
    ]
    args += [conv_w, conv_b, wg, bg, lam]
    grid_spec = pltpu.PrefetchScalarGridSpec(
        num_scalar_prefetch=1,
        grid=(B, nb),
        in_specs=in_specs,
        out_specs=pl.BlockSpec((None, T, 512), lambda b, i, l_ref: (b, pos(i), 0)),
        scratch_shapes=[pltpu.VMEM((1, 512), f32)],
    )
    return pl.pallas_call(
        functools.partial(_lru_kernel, rev=rev, final=final),
        grid_spec=grid_spec,
        out_shape=jax.ShapeDtypeStruct((B, S, 512), bf16 if final else f32),
        compiler_params=_cparams(("arbitrary", "arbitrary")),
        name="lru_fwd" if final else "lru_bwd",
    )(l, *args)


def _diff_kernel(l_ref, laminit_ref, q_ref, k_ref, v_ref, lv_ref, g_ref, o_ref, lhs_ref, m_ref, s_ref, acc_ref):
    tq = q_ref.shape[0]
    tk = k_ref.shape[0]
    h = pl.program_id(1)
    qi = pl.program_id(2)
    ki = pl.program_id(3)
    nk = pl.num_programs(3)

    @pl.when(ki == 0)
    def _():
        qp = q_ref[...] * (HEAD_DIM ** -0.5)
        lo = lax.broadcasted_iota(jnp.int32, (tq, 128), 1) < HEAD_DIM
        lhs_ref[...] = jnp.concatenate([jnp.where(lo, qp, 0.0), jnp.where(lo, 0.0, qp)], axis=0).astype(bf16)
        m_ref[...] = jnp.full_like(m_ref, NEG)
        s_ref[...] = jnp.zeros_like(s_ref)
        acc_ref[...] = jnp.zeros_like(acc_ref)

    s = _dot_nt(lhs_ref[...], k_ref[...].astype(bf16))
    row = lax.broadcasted_iota(jnp.int32, (2 * tq, tk), 0)
    col = lax.broadcasted_iota(jnp.int32, (2 * tq, tk), 1)
    r = jnp.where(row < tq, row, row - tq)
    dist = jnp.abs(r - col + (qi * tq - ki * tk)).astype(f32)
    slope = jnp.where(h == 0, 2.0 ** -2, jnp.where(h == 1, 2.0 ** -4, jnp.where(h == 2, 2.0 ** -6, 2.0 ** -8)))
    s = s - slope * dist
    m_prev = m_ref[...]
    m_new = jnp.maximum(m_prev, jnp.max(s, axis=1, keepdims=True))
    alpha = jnp.exp(m_prev - m_new)
    p = jnp.exp(s - m_new)
    s_ref[...] = alpha * s_ref[...] + jnp.sum(p, axis=1, keepdims=True)
    acc_ref[...] = alpha * acc_ref[...] + _dot(p.astype(bf16), v_ref[...].astype(bf16))
    m_ref[...] = m_new

    @pl.when(ki == nk - 1)
    def _():
        lam_init = laminit_ref[l_ref[0]]
        lv = lv_ref[...]
        lam = (jnp.exp(jnp.sum(lv[0:1] * lv[1:2], axis=1, keepdims=True))
               - jnp.exp(jnp.sum(lv[2:3] * lv[3:4], axis=1, keepdims=True)) + lam_init)
        o = acc_ref[...] / s_ref[...]
        od = o[0:tq] - lam * o[tq:2 * tq]
        var = jnp.mean(od * od, axis=-1, keepdims=True)
        o_ref[...] = (od * lax.rsqrt(var + EPS) * g_ref[...] * (1.0 - lam_init)).astype(o_ref.dtype)


def _diff(l, proj, lam_init_tab, diff_lambda, norm_g):
    B, S, _ = proj.shape
    tq = min(256, S)
    tk = min(512, S)
    grid_spec = pltpu.PrefetchScalarGridSpec(
        num_scalar_prefetch=1,
        grid=(B, DIFF_HEADS, S // tq, S // tk),
        in_specs=[
            pl.BlockSpec(memory_space=pltpu.SMEM),
            pl.BlockSpec((None, tq, 128), lambda b, h, qi, ki, l_ref: (b, qi, OFF_DQ // 128 + h)),
            pl.BlockSpec((None, tk, 128), lambda b, h, qi, ki, l_ref: (b, ki, OFF_DK // 128 + h)),
            pl.BlockSpec((None, tk, 128), lambda b, h, qi, ki, l_ref: (b, ki, OFF_DV // 128 + h)),
            pl.BlockSpec((None, 4, HEAD_DIM), lambda b, h, qi, ki, l_ref: (l_ref[0], 0, 0)),
            pl.BlockSpec((None, 1, 128), lambda b, h, qi, ki, l_ref: (l_ref[0], 0, 0)),
        ],
        out_specs=pl.BlockSpec((None, tq, 128), lambda b, h, qi, ki, l_ref: (b, qi, h)),
        scratch_shapes=[
            pltpu.VMEM((2 * tq, 128), bf16),
            pltpu.VMEM((2 * tq, 1), f32),
            pltpu.VMEM((2 * tq, 1), f32),
            pltpu.VMEM((2 * tq, 128), f32),
        ],
    )
    return pl.pallas_call(
        _diff_kernel,
        grid_spec=grid_spec,
        out_shape=jax.ShapeDtypeStruct((B, S, 512), bf16),
        compiler_params=_cparams(("arbitrary", "arbitrary", "arbitrary", "arbitrary")),
        name="diff_attn",
    )(l, lam_init_tab, proj, proj, proj, diff_lambda, norm_g)


def _outproj_kernel(l_ref, x_ref, oa_ref, ob_ref, oc_ref, od_ref, gate_ref, w_ref, o_ref):
    acc = _dot(oa_ref[...], w_ref[0:512, :])
    acc += _dot(ob_ref[...], w_ref[512:1024, :])
    acc += _dot(oc_ref[...], w_ref[1024:1536, :])
    acc += _dot(od_ref[...], w_ref[1536:2048, :])
    o_ref[...] = x_ref[...] + gate_ref[...] * acc


def _outproj(l, x, oa, ob, oc, od, mod, row0, w_out):
    B, S, _ = x.shape
    tm = min(512, S)
    mix = pl.BlockSpec((None, tm, 512), lambda b, i, l_ref: (b, i, 0))
    grid_spec = pltpu.PrefetchScalarGridSpec(
        num_scalar_prefetch=1,
        grid=(B, S // tm),
        in_specs=[
            pl.BlockSpec((None, tm, D_MODEL), lambda b, i, l_ref: (b, i, 0)),
            mix, mix, mix, mix,
            _mod_spec(row0, 2, 2),
            pl.BlockSpec((None, D_MODEL, D_MODEL), lambda b, i, l_ref: (l_ref[0], 0, 0)),
        ],
        out_specs=pl.BlockSpec((None, tm, D_MODEL), lambda b, i, l_ref: (b, i, 0)),
    )
    return pl.pallas_call(
        _outproj_kernel,
        grid_spec=grid_spec,
        out_shape=jax.ShapeDtypeStruct((B, S, D_MODEL), f32),
        compiler_params=_cparams(("arbitrary", "arbitrary")),
        name="out_proj",
    )(l, x, oa, ob, oc, od, mod, w_out)


def _mlp_kernel(l_ref, x_ref, g_ref, sc_ref, sh_ref, gate_ref, w1_ref, w2_ref, o_ref, h_ref, acc_ref):
    f = pl.program_id(2)

    @pl.when(f == 0)
    def _():
        h_ref[...] = _norm_mod(x_ref[...], g_ref[...], sc_ref[...], sh_ref[...]).astype(bf16)
        acc_ref[...] = jnp.zeros_like(acc_ref)

    t = jnp.maximum(_dot(h_ref[...], w1_ref[...]), 0.0)
    acc_ref[...] += _dot((t * t).astype(bf16), w2_ref[...])

    @pl.when(f == pl.num_programs(2) - 1)
    def _():
        o_ref[...] = x_ref[...] + gate_ref[...] * acc_ref[...]


def _mlp(l, x, norm_g, mod, row0, w1, w2):
    B, S, _ = x.shape
    tm = min(512, S)
    tf = 512
    grid_spec = pltpu.PrefetchScalarGridSpec(
        num_scalar_prefetch=1,
        grid=(B, S // tm, D_FF // tf),
        in_specs=[
            pl.BlockSpec((None, tm, D_MODEL), lambda b, i, f, l_ref: (b, i, 0)),
            pl.BlockSpec((None, 1, D_MODEL), lambda b, i, f, l_ref: (l_ref[0], 0, 0)),
            _mod_spec(row0, 4, 3),
            _mod_spec(row0, 3, 3),
            _mod_spec(row0, 5, 3),
            pl.BlockSpec((None, D_MODEL, tf), lambda b, i, f, l_ref: (l_ref[0], 0, f)),
            pl.BlockSpec((None, tf, D_MODEL), lambda b, i, f, l_ref: (l_ref[0], f, 0)),
        ],
        out_specs=pl.BlockSpec((None, tm, D_MODEL), lambda b, i, f, l_ref: (b, i, 0)),
        scratch_shapes=[pltpu.VMEM((tm, D_MODEL), bf16), pltpu.VMEM((tm, D_MODEL), f32)],
    )
    return pl.pallas_call(
        _mlp_kernel,
        grid_spec=grid_spec,
        out_shape=jax.ShapeDtypeStruct((B, S, D_MODEL), f32),
        compiler_params=_cparams(("arbitrary", "arbitrary", "arbitrary")),
        name="mlp",
    )(l, x, norm_g, mod, mod, mod, w1, w2)


def _final_norm_kernel(x_ref, g_ref, o_ref):
    x = x_ref[...]
    var = jnp.mean(x * x, axis=-1, keepdims=True)
    o_ref[...] = x * lax.rsqrt(var + EPS) * g_ref[...]


def _final_norm(x, g):
    B, S, _ = x.shape
    tm = min(1024, S)
    return pl.pallas_call(
        _final_norm_kernel,
        grid=(B, S // tm),
        in_specs=[
            pl.BlockSpec((None, tm, D_MODEL), lambda b, i: (b, i, 0)),
            pl.BlockSpec((1, D_MODEL), lambda b, i: (0, 0)),
        ],
        out_specs=pl.BlockSpec((None, tm, D_MODEL), lambda b, i: (b, i, 0)),
        out_shape=jax.ShapeDtypeStruct((B, S, D_MODEL), f32),
        compiler_params=_cparams(("arbitrary", "arbitrary")),
        name="final_norm",
    )(x, g.reshape(1, D_MODEL))


def _prepare(norm1_g, norm2_g, w_in, w_out, gdn_conv, gdn_a_log, gdn_dt_bias, gdn_norm_g, swa_sink,
             lru_conv, lru_conv_b, lru_w_gate, lru_b_gate, lru_lambda, diff_lambda, diff_norm_g,
             mlp_w1, mlp_w2):
    depth = w_in.shape[0]
    src = _proj_columns()
    w_p = jnp.where(jnp.asarray(src >= 0)[None, None, :],
                    jnp.take(w_in, jnp.asarray(np.maximum(src, 0)), axis=2), 0.0).astype(bf16)
    pad96 = jnp.zeros((depth, 1, 96), f32)
    alog_row = jnp.concatenate([gdn_a_log.reshape(depth, 1, 16), jnp.zeros((depth, 1, 16), f32), pad96], axis=2)
    dtb_row = jnp.concatenate([gdn_dt_bias.reshape(depth, 1, 16), jnp.zeros((depth, 1, 16), f32), pad96], axis=2)
    wg = jnp.einsum('lrgnde,nm->lrndgme', lru_w_gate, jnp.eye(8, dtype=f32)).reshape(depth, 2, 512, 1024)
    return dict(
        norm1=norm1_g.reshape(depth, 1, D_MODEL),
        norm2=norm2_g.reshape(depth, 1, D_MODEL),
        w_p=w_p,
        w_out=w_out.astype(bf16),
        gdn_conv=gdn_conv,
        alog_row=alog_row,
        dtb_row=dtb_row,
        gdn_ng=jnp.tile(gdn_norm_g, (1, GDN_HEADS)).reshape(depth, 1, 512),
        gdn_consts=_gdn_consts(),
        sink=swa_sink,
        lru_conv=lru_conv,
        lru_conv_b=lru_conv_b.reshape(depth, 1, 512),
        lru_wg=wg.astype(bf16),
        lru_bg=lru_b_gate.reshape(depth, 2, 1, 1024),
        lru_lam=lru_lambda.reshape(depth, 2, 1, 512),
        diff_lambda=diff_lambda,
        diff_ng=diff_norm_g.reshape(depth, 1, 128),
        lam_init=jnp.asarray([0.8 - 0.6 * math.exp(-0.3 * i) for i in range(depth)], f32),
        w1=mlp_w1.astype(bf16),
        w2=mlp_w2.astype(bf16),
    )


def _layer(l, x, mod, row0, p):
    proj = _norm_proj(l, x, p['norm1'], mod, row0, p['w_p'])
    gdn_args = (l, proj, p['gdn_conv'], p['alog_row'], p['dtb_row'], p['gdn_ng'], p['gdn_consts'])
    o_rev = _gdn_pass(*gdn_args, True, None)
    o_a = _gdn_pass(*gdn_args, False, o_rev)
    o_b = _swa(l, proj, p['sink'])
    lru_args = (l, proj, p['lru_conv'], p['lru_conv_b'], p['lru_wg'], p['lru_bg'], p['lru_lam'])
    h_rev = _lru_pass(*lru_args, True, None)
    o_c = _lru_pass(*lru_args, False, h_rev)
    o_d = _diff(l, proj, p['lam_init'], p['diff_lambda'], p['diff_ng'])
    x = _outproj(l, x, o_a, o_b, o_c, o_d, mod, row0, p['w_out'])
    return _mlp(l, x, p['norm2'], mod, row0, p['w1'], p['w2'])


def kernel(x_prompt, x_sample, c_prompt, c_sample, norm1_g, norm2_g, w_mod, b_mod, w_in, w_out, gdn_conv, gdn_a_log, gdn_dt_bias, gdn_norm_g, swa_sink, lru_conv, lru_conv_b, lru_w_gate, lru_b_gate, lru_lambda, diff_lambda, diff_norm_g, mlp_w1, mlp_w2, final_g):
    depth = w_in.shape[0]
    bp, bs = x_prompt.shape[0], x_sample.shape[0]
    assert bp + bs <= MOD_ROWS
    c_all = jnp.concatenate([c_prompt, c_sample, jnp.zeros((MOD_ROWS - bp - bs, D_MODEL), f32)], axis=0)
    mod = _modulation(c_all, w_mod, b_mod).reshape(depth * MOD_ROWS * 6, 1, D_MODEL)
    p = _prepare(norm1_g, norm2_g, w_in, w_out, gdn_conv, gdn_a_log, gdn_dt_bias, gdn_norm_g, swa_sink,
                 lru_conv, lru_conv_b, lru_w_gate, lru_b_gate, lru_lambda, diff_lambda, diff_norm_g,
                 mlp_w1, mlp_w2)

    def body(i, xs):
        l = jnp.full((1,), i, jnp.int32)
        return (_layer(l, xs[0], mod, 0, p), _layer(l, xs[1], mod, bp, p))

    xp, xs = lax.fori_loop(0, depth, body, (x_prompt, x_sample))
    return (_final_norm(xp, final_g), _final_norm(xs, final_g))
```

```python
import functools
import math

import numpy as np
import jax
import jax.numpy as jnp
from jax import lax
from jax.experimental import pallas as pl
from jax.experimental.pallas import tpu as pltpu

f32 = jnp.float32
bf16 = jnp.bfloat16

D_MODEL = 2048
GROUP_W = 512
HEAD_DIM = 64
EPS = 1e-6
GDN_HEADS = 8
GDN_CHUNK = 64
GDN_BLOCK_CHUNKS = 4
SWA_HEADS = 8
SWA_BLOCK = 128
LRU_C = 8.0
DIFF_HEADS = 4
DIFF_TQ = 256
DIFF_TK = 512
DIFF_VROWS = 144
D_FF = 4 * D_MODEL
N_MOD = 6 * D_MODEL
MOD_ROWS = 16

OFF_AQKV, OFF_AZ, OFF_CX, OFF_CG, OFF_AAB = 0, 1536, 2048, 2560, 3072
N_PROJ32 = 3200
OFF_BQ, OFF_DQ, OFF_DK, OFF_DV, OFF_BK, OFF_BV = 0, 512, 1024, 1536, 2048, 2304
N_PROJ16 = 2560
N_PROJ = N_PROJ32 + N_PROJ16
PROJ_TN = 640
NEG = -1e30

VMEM_LIMIT = 56 * 1024 * 1024


def _proj_columns():
    src = np.full((N_PROJ,), -1, np.int64)
    src[0:2048] = np.arange(0, 2048)
    src[OFF_CX:OFF_CX + 512] = np.arange(2848, 3360)
    src[OFF_CG:OFF_CG + 512] = np.arange(3360, 3872)
    src[OFF_AAB:OFF_AAB + 32] = np.arange(2048, 2080)
    o16 = N_PROJ32
    src[o16 + OFF_BQ:o16 + OFF_BQ + 512] = np.arange(2080, 2592)
    src[o16 + OFF_DQ:o16 + OFF_DQ + 512] = np.arange(3872, 4384)
    src[o16 + OFF_DK:o16 + OFF_DK + 512] = np.arange(4384, 4896)
    src[o16 + OFF_DV:o16 + OFF_DV + 512] = np.arange(4896, 5408)
    for kv in range(2):
        for half in range(2):
            o = o16 + 128 * kv + 64 * half
            src[OFF_BK + o:OFF_BK + o + 64] = np.arange(2592 + 64 * kv, 2592 + 64 * kv + 64)
            src[OFF_BV + o:OFF_BV + o + 64] = np.arange(2720 + 64 * kv, 2720 + 64 * kv + 64)
    return src


def _cparams(sem):
    return pltpu.CompilerParams(dimension_semantics=sem, vmem_limit_bytes=VMEM_LIMIT)


def _dot(a, b):
    return jnp.dot(a, b, preferred_element_type=f32)


def _dot_nt(a, b):
    return lax.dot_general(a, b, (((1,), (1,)), ((), ())), preferred_element_type=f32)


def _dot_tn(a, b):
    return lax.dot_general(a, b, (((0,), (0,)), ((), ())), preferred_element_type=f32)


def _split2(x):
    hi = x.astype(bf16)
    return hi, (x - hi.astype(f32)).astype(bf16)


def _dot_sel(x, sel):
    hi, lo = _split2(x)
    return _dot(hi, sel) + _dot(lo, sel)


def _sel_dot(sel, x):
    hi, lo = _split2(x)
    return _dot(sel, hi) + _dot(sel, lo)


def _sigmoid(x):
    return 1.0 / (1.0 + jnp.exp(-x))


def _softplus(x):
    return jnp.maximum(x, 0.0) + jnp.log(1.0 + jnp.exp(-jnp.abs(x)))


def _silu(x):
    return x * _sigmoid(x)


def _gelu_tanh(x):
    return 0.5 * x * (1.0 + jnp.tanh(math.sqrt(2.0 / math.pi) * (x + 0.044715 * (x * x * x))))


def _norm_mod(x, g, sc, sh):
    var = jnp.mean(x * x, axis=-1, keepdims=True)
    return (x * lax.rsqrt(var + EPS) * g) * (1.0 + sc) + sh


def _conv4(x, prev, nxt, w):
    T = x.shape[0]
    row = lax.broadcasted_iota(jnp.int32, x.shape, 0)
    xm2 = pltpu.roll(x, 2, 0)
    xm2 = jnp.where(row == 0, prev[6:7, :], jnp.where(row == 1, prev[7:8, :], xm2))
    xm1 = jnp.where(row == 0, prev[7:8, :], pltpu.roll(x, 1, 0))
    xp1 = jnp.where(row == T - 1, nxt[0:1, :], pltpu.roll(x, T - 1, 0))
    return w[0:1, :] * xm2 + w[1:2, :] * xm1 + w[2:3, :] * x + w[3:4, :] * xp1


def _mod_kernel(c_ref, w_ref, b_ref, o_ref):
    c = c_ref[...]
    o_ref[0] = _dot(_silu(c).astype(bf16), w_ref[0].astype(bf16)) + b_ref[0]


def _modulation(c_all, w_mod, b_mod):
    depth = w_mod.shape[0]
    tn = 1024
    return pl.pallas_call(
        _mod_kernel,
        grid=(depth, N_MOD // tn),
        in_specs=[
            pl.BlockSpec((MOD_ROWS, D_MODEL), lambda l, j: (0, 0)),
            pl.BlockSpec((1, D_MODEL, tn), lambda l, j: (l, 0, j)),
            pl.BlockSpec((1, 1, tn), lambda l, j: (l, 0, j)),
        ],
        out_specs=pl.BlockSpec((1, MOD_ROWS, tn), lambda l, j: (l, 0, j)),
        out_shape=jax.ShapeDtypeStruct((depth, MOD_ROWS, N_MOD), f32),
        compiler_params=_cparams(("arbitrary", "arbitrary")),
        name="modulation",
    )(c_all, w_mod, b_mod.reshape(depth, 1, N_MOD))


def _mod_spec(row0, chunk, ngrid):
    def imap(*a):
        b, l_ref = a[0], a[ngrid]
        return (l_ref[0] * (MOD_ROWS * 6) + (row0 + b) * 6 + chunk, 0, 0)
    return pl.BlockSpec((None, 1, D_MODEL), imap)


N_TILES32 = N_PROJ32 // PROJ_TN


def _norm_proj_kernel(l_ref, x_ref, g_ref, sc_ref, sh_ref, w_ref, o32_ref, o16_ref, h_ref):
    j = pl.program_id(2)

    @pl.when(j == 0)
    def _():
        h_ref[...] = _norm_mod(x_ref[...], g_ref[...], sc_ref[...], sh_ref[...]).astype(bf16)

    r = _dot(h_ref[...], w_ref[...])

    @pl.when(j < N_TILES32)
    def _():
        o32_ref[...] = r

    @pl.when(j >= N_TILES32)
    def _():
        o16_ref[...] = r.astype(bf16)


def _norm_proj(l, x, norm_g, mod, row0, w_p):
    B, S, _ = x.shape
    tm = min(1024, S)
    grid_spec = pltpu.PrefetchScalarGridSpec(
        num_scalar_prefetch=1,
        grid=(B, S // tm, N_PROJ // PROJ_TN),
        in_specs=[
            pl.BlockSpec((None, tm, D_MODEL), lambda b, i, j, l_ref: (b, i, 0)),
            pl.BlockSpec((None, 1, D_MODEL), lambda b, i, j, l_ref: (l_ref[0], 0, 0)),
            _mod_spec(row0, 1, 3),
            _mod_spec(row0, 0, 3),
            pl.BlockSpec((None, D_MODEL, PROJ_TN), lambda b, i, j, l_ref: (l_ref[0], 0, j)),
        ],
        out_specs=[
            pl.BlockSpec((None, tm, PROJ_TN), lambda b, i, j, l_ref: (b, i, jnp.minimum(j, N_TILES32 - 1))),
            pl.BlockSpec((None, tm, PROJ_TN), lambda b, i, j, l_ref: (b, i, jnp.maximum(j - N_TILES32, 0))),
        ],
        scratch_shapes=[pltpu.VMEM((tm, D_MODEL), bf16)],
    )
    return pl.pallas_call(
        _norm_proj_kernel,
        grid_spec=grid_spec,
        out_shape=[jax.ShapeDtypeStruct((B, S, N_PROJ32), f32), jax.ShapeDtypeStruct((B, S, N_PROJ16), bf16)],
        compiler_params=_cparams(("arbitrary", "arbitrary", "arbitrary")),
        name="norm_proj",
    )(l, x, norm_g, mod, mod, w_p)


def _gdn_kernel(l_ref, x_ref, xp_ref, xn_ref, ab_ref, *rest, rev, final):
    if final:
        (z_ref, ob_ref, cw_ref, alog_ref, dtb_ref, ng_ref, ehead_ref, expg_ref, expb_ref,
         o_ref, s_ref) = rest
    else:
        (cw_ref, alog_ref, dtb_ref, ehead_ref, expg_ref, expb_ref, o_ref, s_ref) = rest
    C = GDN_CHUNK
    TB = x_ref.shape[0]
    NC = TB // C
    i = pl.program_id(1)
    nb = pl.num_programs(1)
    blk = nb - 1 - i if rev else i

    @pl.when(i == 0)
    def _():
        s_ref[...] = jnp.zeros_like(s_ref)

    prev = jnp.where(blk > 0, xp_ref[...], 0.0)
    nxt = jnp.where(blk < nb - 1, xn_ref[...], 0.0)
    y = _silu(_conv4(x_ref[...], prev, nxt, cw_ref[...]))
    q, k, v = y[:, 0:512], y[:, 512:1024], y[:, 1024:1536]
    ehead = ehead_ref[...]
    qn = q * lax.rsqrt(_dot((q * q).astype(bf16), ehead) + EPS) * (HEAD_DIM ** -0.5)
    kn = k * lax.rsqrt(_dot((k * k).astype(bf16), ehead) + EPS)

    gin = ab_ref[...]
    g = -jnp.exp(alog_ref[...]) * _softplus(gin + dtb_ref[...])
    rt = lax.broadcasted_iota(jnp.int32, (TB, TB), 0)
    ct = lax.broadcasted_iota(jnp.int32, (TB, TB), 1)
    same = (rt // C) == (ct // C)
    ones_bd = jnp.where(same, 1.0, 0.0).astype(bf16)
    cum_bd = jnp.where(same & ((ct >= rt) if rev else (ct <= rt)), 1.0, 0.0).astype(bf16)
    gc = _sel_dot(cum_bd, g)
    gl = _sel_dot(ones_bd, g)
    expg = expg_ref[...]
    gc_x = _dot_sel(gc, expg)
    gl_x = _dot_sel(gl, expg)
    beta_x = _sigmoid(_dot_sel(gin, expb_ref[...]))
    egc_x = jnp.exp(gc_x)
    etail_x = jnp.exp(gl_x - gc_x)
    egl_x = jnp.exp(gl_x)

    kb = kn * beta_x
    vb = v * beta_x
    kw = kb * egc_x
    qh = qn * egc_x
    kt = kn * etail_x

    W = 4 * C
    r2 = lax.broadcasted_iota(jnp.int32, (W, W), 0) // C
    c2 = lax.broadcasted_iota(jnp.int32, (W, W), 1) // C
    bd = r2 == c2
    rs = lax.broadcasted_iota(jnp.int32, (C, W), 0)
    js = lax.broadcasted_iota(jnp.int32, (C, W), 1) % C
    tri = (js >= rs) if rev else (js <= rs)
    strict = (js > rs) if rev else (js < rs)
    eye_s = jnp.where(js == rs, 1.0, 0.0)
    ones_cc = jnp.ones((C, C), bf16)

    def blockdiag(m):
        mb = m.astype(bf16)
        return jnp.where(bd, jnp.concatenate([mb, mb, mb, mb], axis=0), jnp.zeros((), bf16))

    def _dot_hi(x, y):
        xh = x.astype(bf16)
        xl = (x - xh.astype(f32)).astype(bf16)
        yh = y.astype(bf16)
        bh = blockdiag(yh)
        return _dot(xh, bh) + _dot(xl, bh) + _dot(xh, blockdiag(y - yh.astype(f32)))

    pre = {}
    for c in range(NC):
        rw = slice(C * c, C * c + C)
        for grp in range(2):
            sl = slice(W * grp, W * grp + W)
            kst = blockdiag(kn[rw, sl])
            kk = _dot_nt(kb[rw, sl].astype(bf16), kst)
            qk = _dot_nt(qn[rw, sl].astype(bf16), kst)
            col = gc_x[rw, sl]
            rowv = _sel_dot(ones_cc, col * eye_s)
            dm = jnp.exp(jnp.where(tri, col - rowv, NEG))
            a = jnp.where(strict, kk * dm, 0.0)
            t = eye_s - jnp.where(rs // 2 == js // 2, a, 0.0)
            s = 2
            while s < C:
                a_off = jnp.where((rs // (2 * s) == js // (2 * s)) & (rs // s != js // s), a, 0.0)
                y = _dot_hi(t, a_off)
                t = t - _dot_hi(y, t)
                s *= 2
            tb = t.astype(bf16)
            u = _dot(tb, blockdiag(vb[rw, sl]))
            w = _dot(tb, blockdiag(kw[rw, sl]))
            pre[(c, grp)] = (u, w, (qk * dm).astype(bf16))

    rows = [None] * NC
    order = range(NC - 1, -1, -1) if rev else range(NC)
    sts = [s_ref[0], s_ref[1]]
    for c in order:
        rw = slice(C * c, C * c + C)
        outs = []
        for grp in range(2):
            sl = slice(W * grp, W * grp + W)
            u, w, intra = pre[(c, grp)]
            st = sts[grp]
            wq = jnp.concatenate([w, qh[rw, sl]], axis=0).astype(bf16)
            ws = _dot(wq, st.astype(bf16))
            vnew = u - ws[0:C]
            outs.append(ws[C:2 * C] + _dot(intra, blockdiag(vnew)))
            upd = _dot_tn(kt[rw, sl].astype(bf16), vnew.astype(bf16))
            sts[grp] = st * egl_x[C * c:C * c + 1, sl] + jnp.where(bd, upd, 0.0)
        rows[c] = jnp.concatenate(outs, axis=1)
    s_ref[0] = sts[0]
    s_ref[1] = sts[1]
    o_full = jnp.concatenate(rows, axis=0)
    if final:
        tot = o_full + ob_ref[...]
        ms = _dot((tot * tot).astype(bf16), ehead) * (1.0 / HEAD_DIM)
        o_ref[...] = (tot * lax.rsqrt(ms + EPS) * ng_ref[...] * _silu(z_ref[...])).astype(o_ref.dtype)
    else:
        o_ref[...] = o_full


def _gdn_consts():
    lane = np.arange(512)
    ehead = (lane[:, None] // HEAD_DIM == lane[None, :] // HEAD_DIM).astype(np.float32)
    expg = np.zeros((2, 128, 512), np.float32)
    expb = np.zeros((2, 128, 512), np.float32)
    for r in range(2):
        expg[r, 8 * r + lane // HEAD_DIM, lane] = 1.0
        expb[r, 16 + 8 * r + lane // HEAD_DIM, lane] = 1.0
    return jnp.asarray(ehead, bf16), jnp.asarray(expg, bf16), jnp.asarray(expb, bf16)


def _gdn_pass(l, proj, conv_w, alog_row, dtb_row, ng_row, consts, rev, ob):
    B, S, _ = proj.shape
    TB = min(GDN_BLOCK_CHUNKS * GDN_CHUNK, S)
    nb = S // TB
    n8 = S // 8
    final = not rev
    ehead, expg, expb = consts

    def pos(i):
        return nb - 1 - i if rev else i

    in_specs = [
        pl.BlockSpec((None, TB, 1536), lambda b, i, l_ref: (b, pos(i), 0)),
        pl.BlockSpec((None, 8, 1536), lambda b, i, l_ref: (b, jnp.maximum(pos(i) * (TB // 8) - 1, 0), 0)),
        pl.BlockSpec((None, 8, 1536), lambda b, i, l_ref: (b, jnp.minimum((pos(i) + 1) * (TB // 8), n8 - 1), 0)),
        pl.BlockSpec((None, TB, 128), lambda b, i, l_ref: (b, pos(i), OFF_AAB // 128)),
    ]
    args = [proj, proj, proj, proj]
    if final:
        in_specs += [
            pl.BlockSpec((None, TB, 512), lambda b, i, l_ref: (b, pos(i), OFF_AZ // 512)),
            pl.BlockSpec((None, TB, 512), lambda b, i, l_ref: (b, pos(i), 0)),
        ]
        args += [proj, ob]
    in_specs += [
        pl.BlockSpec((None, 4, 1536), lambda b, i, l_ref: (l_ref[0], 0, 0)),
        pl.BlockSpec((None, 1, 128), lambda b, i, l_ref: (l_ref[0], 0, 0)),
        pl.BlockSpec((None, 1, 128), lambda b, i, l_ref: (l_ref[0], 0, 0)),
    ]
    args += [conv_w, alog_row, dtb_row]
    if final:
        in_specs.append(pl.BlockSpec((None, 1, 512), lambda b, i, l_ref: (l_ref[0], 0, 0)))
        args.append(ng_row)
    in_specs += [
        pl.BlockSpec((512, 512), lambda b, i, l_ref: (0, 0)),
        pl.BlockSpec((None, 128, 512), lambda b, i, l_ref: (1 if rev else 0, 0, 0)),
        pl.BlockSpec((None, 128, 512), lambda b, i, l_ref: (1 if rev else 0, 0, 0)),
    ]
    args += [ehead, expg, expb]
    grid_spec = pltpu.PrefetchScalarGridSpec(
        num_scalar_prefetch=1,
        grid=(B, nb),
        in_specs=in_specs,
        out_specs=pl.BlockSpec((None, TB, 512), lambda b, i, l_ref: (b, pos(i), 0)),
        scratch_shapes=[pltpu.VMEM((2, 4 * GDN_CHUNK, 4 * GDN_CHUNK), f32)],
    )
    return pl.pallas_call(
        functools.partial(_gdn_kernel, rev=rev, final=final),
        grid_spec=grid_spec,
        out_shape=jax.ShapeDtypeStruct((B, S, 512), bf16 if final else f32),
        compiler_params=_cparams(("arbitrary", "arbitrary")),
        name="gdn_fwd" if final else "gdn_bwd",
    )(l, *args)


def _swa_kernel(l_ref, sink_ref, q_ref, kp_ref, kc_ref, kn_ref, vp_ref, vc_ref, vn_ref, o_ref):
    T = SWA_BLOCK
    i = pl.program_id(1)
    nb = pl.num_programs(1)
    l = l_ref[0]
    row = lax.broadcasted_iota(jnp.int32, (2 * T, 3 * T), 0)
    col = lax.broadcasted_iota(jnp.int32, (2 * T, 3 * T), 1)
    top = row < T
    r = jnp.where(top, row, row - T)
    disti = jnp.abs(r + T - col)
    dist = disti.astype(f32)
    valid = (disti <= SWA_BLOCK) & ((col >= T) | (i > 0)) & ((col < 2 * T) | (i < nb - 1))
    lo = lax.broadcasted_iota(jnp.int32, (T, 128), 1) < HEAD_DIM
    top1 = lax.broadcasted_iota(jnp.int32, (2 * T, 1), 0) < T
    for kv in range(2):
        ksl = slice(128 * kv, 128 * kv + 128)
        kd = jnp.concatenate([kp_ref[:, ksl], kc_ref[:, ksl], kn_ref[:, ksl]], axis=0)
        vd = jnp.concatenate([vp_ref[:, ksl], vc_ref[:, ksl], vn_ref[:, ksl]], axis=0)
        for pp in range(2):
            p = 2 * kv + pp
            qp = q_ref[:, 128 * p:128 * p + 128].astype(f32) * (HEAD_DIM ** -0.5)
            lhs = jnp.concatenate([jnp.where(lo, qp, 0.0), jnp.where(lo, 0.0, qp)], axis=0).astype(bf16)
            s = _dot_nt(lhs, kd)
            slope = jnp.where(top, 2.0 ** -(2 * p + 1), 2.0 ** -(2 * p + 2))
            s = jnp.where(valid, s - slope * dist, NEG)
            sk = jnp.where(top1, sink_ref[l, 2 * p], sink_ref[l, 2 * p + 1])
            m = jnp.maximum(jnp.max(s, axis=1, keepdims=True), sk)
            e = jnp.exp(s - m)
            den = jnp.sum(e, axis=1, keepdims=True) + jnp.exp(sk - m)
            o = _dot(e.astype(bf16), vd) / den
            o_ref[:, 128 * p:128 * p + 128] = jnp.where(lo, o[0:T], o[T:2 * T]).astype(o_ref.dtype)


def _swa(l, proj16, sink):
    B, S, _ = proj16.shape
    T = SWA_BLOCK
    nb = S // T
    kb, vb = OFF_BK // 256, OFF_BV // 256

    def side(blk, which):
        if which < 0:
            return lambda b, i, l_ref: (b, jnp.maximum(i - 1, 0), blk)
        if which > 0:
            return lambda b, i, l_ref: (b, jnp.minimum(i + 1, nb - 1), blk)
        return lambda b, i, l_ref: (b, i, blk)

    grid_spec = pltpu.PrefetchScalarGridSpec(
        num_scalar_prefetch=1,
        grid=(B, nb),
        in_specs=[
            pl.BlockSpec(memory_space=pltpu.SMEM),
            pl.BlockSpec((None, T, 512), lambda b, i, l_ref: (b, i, OFF_BQ // 512)),
            pl.BlockSpec((None, T, 256), side(kb, -1)),
            pl.BlockSpec((None, T, 256), side(kb, 0)),
            pl.BlockSpec((None, T, 256), side(kb, 1)),
            pl.BlockSpec((None, T, 256), side(vb, -1)),
            pl.BlockSpec((None, T, 256), side(vb, 0)),
            pl.BlockSpec((None, T, 256), side(vb, 1)),
        ],
        out_specs=pl.BlockSpec((None, T, 512), lambda b, i, l_ref: (b, i, 0)),
    )
    return pl.pallas_call(
        _swa_kernel,
        grid_spec=grid_spec,
        out_shape=jax.ShapeDtypeStruct((B, S, 512), bf16),
        compiler_params=_cparams(("arbitrary", "arbitrary")),
        name="swa",
    )(l, sink, proj16, proj16, proj16, proj16, proj16, proj16, proj16)


def _lru_kernel(l_ref, x_ref, xp_ref, xn_ref, *rest, rev, final):
    if final:
        (gate_ref, hb_ref, cw_ref, cb_ref, wg_ref, bg_ref, lam_ref, o_ref, carry_ref) = rest
    else:
        (cw_ref, cb_ref, wg_ref, bg_ref, lam_ref, o_ref, carry_ref) = rest
    T = x_ref.shape[0]
    i = pl.program_id(1)
    nb = pl.num_programs(1)
    blk = nb - 1 - i if rev else i

    @pl.when(i == 0)
    def _():
        carry_ref[...] = jnp.zeros_like(carry_ref)

    prev = jnp.where(blk > 0, xp_ref[...], 0.0)
    nxt = jnp.where(blk < nb - 1, xn_ref[...], 0.0)
    xc = _conv4(x_ref[...], prev, nxt, cw_ref[...]) + cb_ref[...]
    gates = _sigmoid(_dot(xc.astype(bf16), wg_ref[...]) + bg_ref[...])
    rg, ig = gates[:, 0:512], gates[:, 512:1024]
    log_a = (-LRU_C) * rg * _softplus(-lam_ref[...])
    a = jnp.exp(log_a)
    y2 = 2.0 * log_a
    one_minus_a2 = -jnp.tanh(0.5 * y2) * (jnp.exp(y2) + 1.0)
    bv = jnp.sqrt(one_minus_a2) * ig * xc
    row = lax.broadcasted_iota(jnp.int32, (T, 512), 0)
    s = 1
    while s < T:
        if rev:
            ok = row < T - s
            a_s, b_s = pltpu.roll(a, T - s, 0), pltpu.roll(bv, T - s, 0)
        else:
            ok = row >= s
            a_s, b_s = pltpu.roll(a, s, 0), pltpu.roll(bv, s, 0)
        bv = jnp.where(ok, a * b_s + bv, bv)
        a = jnp.where(ok, a * a_s, a)
        s *= 2
    h = a * carry_ref[...] + bv
    carry_ref[...] = h[0:1, :] if rev else h[T - 1:T, :]
    if final:
        o_ref[...] = ((h + hb_ref[...]) * _gelu_tanh(gate_ref[...])).astype(o_ref.dtype)
    else:
        o_ref[...] = h


def _lru_pass(l, proj, conv_w, conv_b, wg, bg, lam, rev, hb):
    B, S, _ = proj.shape
    T = min(256, S)
    nb = S // T
    n8 = S // 8
    final = not rev
    d = 1 if rev else 0
    xb = OFF_CX // 512

    def pos(i):
        return nb - 1 - i if rev else i

    in_specs = [
        pl.BlockSpec((None, T, 512), lambda b, i, l_ref: (b, pos(i), xb)),
        pl.BlockSpec((None, 8, 512), lambda b, i, l_ref: (b, jnp.maximum(pos(i) * (T // 8) - 1, 0), xb)),
        pl.BlockSpec((None, 8, 512), lambda b, i, l_ref: (b, jnp.minimum((pos(i) + 1) * (T // 8), n8 - 1), xb)),
    ]
    args = [proj, proj, proj]
    if final:
        in_specs += [
            pl.BlockSpec((None, T, 512), lambda b, i, l_ref: (b, pos(i), OFF_CG // 512)),
            pl.BlockSpec((None, T, 512), lambda b, i, l_ref: (b, pos(i), 0)),
        ]
        args += [proj, hb]
    in_specs += [
        pl.BlockSpec((None, 4, 512), lambda b, i, l_ref: (l_ref[0], 0, 0)),
        pl.BlockSpec((None, 1, 512), lambda b, i, l_ref: (l_ref[0], 0, 0)),
        pl.BlockSpec((None, None, 512, 1024), lambda b, i, l_ref: (l_ref[0], d, 0, 0)),
        pl.BlockSpec((None, None, 1, 1024), lambda b, i, l_ref: (l_ref[0], d, 0, 0)),
        pl.BlockSpec((None, None, 1, 512), lambda b, i, l_ref: (l_ref[0], d, 0, 0)),
    ]
    args += [conv_w, conv_b, wg, bg, lam]
    grid_spec = pltpu.PrefetchScalarGridSpec(
        num_scalar_prefetch=1,
        grid=(B, nb),
        in_specs=in_specs,
        out_specs=pl.BlockSpec((None, T, 512), lambda b, i, l_ref: (b, pos(i), 0)),
        scratch_shapes=[pltpu.VMEM((1, 512), f32)],
    )
    return pl.pallas_call(
        functools.partial(_lru_kernel, rev=rev, final=final),
        grid_spec=grid_spec,
        out_shape=jax.ShapeDtypeStruct((B, S, 512), bf16 if final else f32),
        compiler_params=_cparams(("arbitrary", "arbitrary")),
        name="lru_fwd" if final else "lru_bwd",
    )(l, *args)


def _diff_feat_consts(tk):
    c = np.arange(tk)
    feat = np.zeros((tk, 128), np.float32)
    feat[:, 0] = 1.0
    feat[:, 1] = c % 256
    feat[:, 2] = c // 256
    return jnp.asarray(feat, bf16)


def _diff_kernel(l_ref, laminit_ref, q_ref, k_ref, vt_ref, feat_ref, lv_ref, g_ref, o_ref,
                 lhs_ref, m_ref, alpha_ref, acc_ref, sbuf0_ref, sbuf1_ref, pbuf0_ref, pbuf1_ref):
    sbufs = (sbuf0_ref, sbuf1_ref)
    pbufs = (pbuf0_ref, pbuf1_ref)
    tq = q_ref.shape[0]
    nk, _, tk = vt_ref.shape
    assert nk % 2 == 0
    RB = 64
    h = pl.program_id(1)
    qi = pl.program_id(2)
    q0 = qi * tq
    slope = jnp.where(h == 0, 2.0 ** -2, jnp.where(h == 1, 2.0 ** -4, jnp.where(h == 2, 2.0 ** -6, 2.0 ** -8)))

    qp = q_ref[...].astype(f32) * (HEAD_DIM ** -0.5)
    lane = lax.broadcasted_iota(jnp.int32, (2 * tq, 128), 1)
    rr = lax.broadcasted_iota(jnp.int32, (2 * tq, 128), 0)
    rr = jnp.where(rr < tq, rr, rr - tq).astype(f32)
    lo = lax.broadcasted_iota(jnp.int32, (tq, 128), 1) < HEAD_DIM
    qm = jnp.concatenate([jnp.where(lo, qp, 0.0), jnp.where(lo, 0.0, qp)], axis=0).astype(bf16)
    fplus = jnp.where(lane == 0, -slope * rr,
                      jnp.where(lane == 1, slope, jnp.where(lane == 2, 256.0 * slope, 0.0)))
    lhs_ref[0] = jnp.concatenate([qm, fplus.astype(bf16)], axis=1)
    lhs_ref[1] = jnp.concatenate([qm, (-fplus).astype(bf16)], axis=1)
    lhs_ref[2] = jnp.concatenate([qm, jnp.zeros((2 * tq, 128), bf16)], axis=1)
    m_ref[...] = jnp.full_like(m_ref, NEG)
    alpha_ref[...] = jnp.ones_like(alpha_ref)
    acc_ref[...] = jnp.zeros_like(acc_ref)
    pbuf1_ref[...] = jnp.zeros_like(pbuf1_ref)

    def variant(ki):
        d = q0 - ki * tk
        return d, jnp.where(d >= tk, 0, jnp.where(d <= -tq, 1, 2))

    def scores(ki, slot):
        ki = jnp.minimum(ki, nk - 1)
        _, var = variant(ki)
        kx = jnp.concatenate([k_ref[pl.ds(pl.multiple_of(ki * tk, tk), tk), :], feat_ref[...]], axis=1)
        sbufs[slot][...] = _dot_nt(kx, lhs_ref[var])

    def accumulate(ki, slot):
        acc_ref[...] = alpha_ref[slot] * acc_ref[...] + _dot(vt_ref[ki], pbufs[slot][...])

    def softmax(ki, slot):
        d, var = variant(ki)
        mx = jnp.full((8, 2 * tq), NEG, f32)
        for rb in range(tk // RB):
            sb = sbufs[slot][rb * RB:rb * RB + RB, :]
            for r8 in range(RB // 8):
                mx = jnp.maximum(mx, sb[8 * r8:8 * r8 + 8, :])
        dabs = jnp.abs(jnp.zeros((1, 2 * tq), jnp.int32) + d).astype(f32)
        cd = jnp.where(var == 2, 0.0, slope * dabs)
        m_prev = m_ref[...]
        m_new = jnp.maximum(m_prev, jnp.max(mx, axis=0, keepdims=True) - cd)
        alpha_ref[slot] = jnp.exp(m_prev - m_new)
        m_ref[...] = m_new
        shift = m_new + cd
        for rb in range(tk // RB):
            rows = slice(rb * RB, rb * RB + RB)
            pbufs[slot][rows, :] = jnp.exp(sbufs[slot][rows, :] - shift).astype(bf16)

    def diag_bias(ki, slot):
        d, var = variant(ki)

        @pl.when(var == 2)
        def _():
            for rb in range(tk // RB):
                rows = slice(rb * RB, rb * RB + RB)
                ci = lax.broadcasted_iota(jnp.int32, (RB, 2 * tq), 0) + (rb * RB)
                ji = lax.broadcasted_iota(jnp.int32, (RB, 2 * tq), 1)
                r = jnp.where(ji < tq, ji, ji - tq)
                sbufs[slot][rows, :] = sbufs[slot][rows, :] - slope * jnp.abs(r - ci + d).astype(f32)

    scores(0, 0)

    def body(j, carry):
        ka = 2 * j
        diag_bias(ka, 0)
        softmax(ka, 0)
        scores(ka + 1, 1)
        accumulate(jnp.maximum(ka - 1, 0), 1)
        diag_bias(ka + 1, 1)
        softmax(ka + 1, 1)
        scores(ka + 2, 0)
        accumulate(ka, 0)
        return carry

    lax.fori_loop(0, nk // 2, body, 0)
    accumulate(nk - 1, 1)

    lam_init = laminit_ref[l_ref[0]]
    lv = lv_ref[...]
    lam = (jnp.exp(jnp.sum(lv[0:1] * lv[1:2], axis=1, keepdims=True))
           - jnp.exp(jnp.sum(lv[2:3] * lv[3:4], axis=1, keepdims=True)) + lam_init)
    acc = acc_ref[...]
    ot = acc[0:128, :] / acc[128:129, :]
    od = (ot[:, 0:tq] - lam * ot[:, tq:2 * tq]).T
    var = jnp.mean(od * od, axis=-1, keepdims=True)
    o_ref[...] = (od * lax.rsqrt(var + EPS) * g_ref[...] * (1.0 - lam_init)).astype(o_ref.dtype)


def _diff(l, proj16, lam_init_tab, diff_lambda, norm_g):
    B, S, _ = proj16.shape
    tq = min(DIFF_TQ, S)
    tk = min(DIFF_TK, S // 2)
    nk = S // tk
    vt = proj16[:, :, OFF_DV:OFF_DV + 512].reshape(B, nk, tk, DIFF_HEADS, 128).transpose(0, 3, 1, 4, 2)
    extra = jnp.zeros((B, DIFF_HEADS, nk, DIFF_VROWS - 128, tk), bf16).at[:, :, :, 0, :].set(1.0)
    vt = jnp.concatenate([vt, extra], axis=3)
    grid_spec = pltpu.PrefetchScalarGridSpec(
        num_scalar_prefetch=1,
        grid=(B, DIFF_HEADS, S // tq),
        in_specs=[
            pl.BlockSpec(memory_space=pltpu.SMEM),
            pl.BlockSpec((None, tq, 128), lambda b, h, qi, l_ref: (b, qi, OFF_DQ // 128 + h)),
            pl.BlockSpec((None, S, 128), lambda b, h, qi, l_ref: (b, 0, OFF_DK // 128 + h)),
            pl.BlockSpec((None, None, nk, DIFF_VROWS, tk), lambda b, h, qi, l_ref: (b, h, 0, 0, 0)),
            pl.BlockSpec((tk, 128), lambda b, h, qi, l_ref: (0, 0)),
            pl.BlockSpec((None, 4, HEAD_DIM), lambda b, h, qi, l_ref: (l_ref[0], 0, 0)),
            pl.BlockSpec((None, 1, 128), lambda b, h, qi, l_ref: (l_ref[0], 0, 0)),
        ],
        out_specs=pl.BlockSpec((None, tq, 128), lambda b, h, qi, l_ref: (b, qi, h)),
        scratch_shapes=[
            pltpu.VMEM((3, 2 * tq, 256), bf16),
            pltpu.VMEM((1, 2 * tq), f32),
            pltpu.VMEM((2, 1, 2 * tq), f32),
            pltpu.VMEM((DIFF_VROWS, 2 * tq), f32),
            pltpu.VMEM((tk, 2 * tq), f32),
            pltpu.VMEM((tk, 2 * tq), f32),
            pltpu.VMEM((tk, 2 * tq), bf16),
            pltpu.VMEM((tk, 2 * tq), bf16),
        ],
    )
    return pl.pallas_call(
        _diff_kernel,
        grid_spec=grid_spec,
        out_shape=jax.ShapeDtypeStruct((B, S, 512), bf16),
        compiler_params=_cparams(("arbitrary", "arbitrary", "arbitrary")),
        name="diff_attn",
    )(l, lam_init_tab, proj16, proj16, vt, _diff_feat_consts(tk), diff_lambda, norm_g)


def _outproj_kernel(l_ref, x_ref, oa_ref, ob_ref, oc_ref, od_ref, gate_ref, w_ref, o_ref):
    acc = _dot(oa_ref[...], w_ref[0:512, :])
    acc += _dot(ob_ref[...], w_ref[512:1024, :])
    acc += _dot(oc_ref[...], w_ref[1024:1536, :])
    acc += _dot(od_ref[...], w_ref[1536:2048, :])
    o_ref[...] = x_ref[...] + gate_ref[...] * acc


def _outproj(l, x, oa, ob, oc, od, mod, row0, w_out):
    B, S, _ = x.shape
    tm = min(512, S)
    mix = pl.BlockSpec((None, tm, 512), lambda b, i, l_ref: (b, i, 0))
    grid_spec = pltpu.PrefetchScalarGridSpec(
        num_scalar_prefetch=1,
        grid=(B, S // tm),
        in_specs=[
            pl.BlockSpec((None, tm, D_MODEL), lambda b, i, l_ref: (b, i, 0)),
            mix, mix, mix, mix,
            _mod_spec(row0, 2, 2),
            pl.BlockSpec((None, D_MODEL, D_MODEL), lambda b, i, l_ref: (l_ref[0], 0, 0)),
        ],
        out_specs=pl.BlockSpec((None, tm, D_MODEL), lambda b, i, l_ref: (b, i, 0)),
    )
    return pl.pallas_call(
        _outproj_kernel,
        grid_spec=grid_spec,
        out_shape=jax.ShapeDtypeStruct((B, S, D_MODEL), f32),
        compiler_params=_cparams(("arbitrary", "arbitrary")),
        name="out_proj",
    )(l, x, oa, ob, oc, od, mod, w_out)


def _mlp_kernel(l_ref, x_ref, g_ref, sc_ref, sh_ref, gate_ref, w1_ref, w2_ref, o_ref, h_ref, acc_ref):
    f = pl.program_id(2)

    @pl.when(f == 0)
    def _():
        h_ref[...] = _norm_mod(x_ref[...], g_ref[...], sc_ref[...], sh_ref[...]).astype(bf16)
        acc_ref[...] = jnp.zeros_like(acc_ref)

    t = jnp.maximum(_dot(h_ref[...], w1_ref[...]), 0.0)
    acc_ref[...] += _dot((t * t).astype(bf16), w2_ref[...])

    @pl.when(f == pl.num_programs(2) - 1)
    def _():
        o_ref[...] = x_ref[...] + gate_ref[...] * acc_ref[...]


def _mlp(l, x, norm_g, mod, row0, w1, w2):
    B, S, _ = x.shape
    tm = min(512, S)
    tf = 512
    grid_spec = pltpu.PrefetchScalarGridSpec(
        num_scalar_prefetch=1,
        grid=(B, S // tm, D_FF // tf),
        in_specs=[
            pl.BlockSpec((None, tm, D_MODEL), lambda b, i, f, l_ref: (b, i, 0)),
            pl.BlockSpec((None, 1, D_MODEL), lambda b, i, f, l_ref: (l_ref[0], 0, 0)),
            _mod_spec(row0, 4, 3),
            _mod_spec(row0, 3, 3),
            _mod_spec(row0, 5, 3),
            pl.BlockSpec((None, D_MODEL, tf), lambda b, i, f, l_ref: (l_ref[0], 0, f)),
            pl.BlockSpec((None, tf, D_MODEL), lambda b, i, f, l_ref: (l_ref[0], f, 0)),
        ],
        out_specs=pl.BlockSpec((None, tm, D_MODEL), lambda b, i, f, l_ref: (b, i, 0)),
        scratch_shapes=[pltpu.VMEM((tm, D_MODEL), bf16), pltpu.VMEM((tm, D_MODEL), f32)],
    )
    return pl.pallas_call(
        _mlp_kernel,
        grid_spec=grid_spec,
        out_shape=jax.ShapeDtypeStruct((B, S, D_MODEL), f32),
        compiler_params=_cparams(("arbitrary", "arbitrary", "arbitrary")),
        name="mlp",
    )(l, x, norm_g, mod, mod, mod, w1, w2)


def _final_norm_kernel(x_ref, g_ref, o_ref):
    x = x_ref[...]
    var = jnp.mean(x * x, axis=-1, keepdims=True)
    o_ref[...] = x * lax.rsqrt(var + EPS) * g_ref[...]


def _final_norm(x, g):
    B, S, _ = x.shape
    tm = min(1024, S)
    return pl.pallas_call(
        _final_norm_kernel,
        grid=(B, S // tm),
        in_specs=[
            pl.BlockSpec((None, tm, D_MODEL), lambda b, i: (b, i, 0)),
            pl.BlockSpec((1, D_MODEL), lambda b, i: (0, 0)),
        ],
        out_specs=pl.BlockSpec((None, tm, D_MODEL), lambda b, i: (b, i, 0)),
        out_shape=jax.ShapeDtypeStruct((B, S, D_MODEL), f32),
        compiler_params=_cparams(("arbitrary", "arbitrary")),
        name="final_norm",
    )(x, g.reshape(1, D_MODEL))


def _prepare(norm1_g, norm2_g, w_in, w_out, gdn_conv, gdn_a_log, gdn_dt_bias, gdn_norm_g, swa_sink,
             lru_conv, lru_conv_b, lru_w_gate, lru_b_gate, lru_lambda, diff_lambda, diff_norm_g,
             mlp_w1, mlp_w2):
    depth = w_in.shape[0]
    src = _proj_columns()
    w_p = jnp.where(jnp.asarray(src >= 0)[None, None, :],
                    jnp.take(w_in, jnp.asarray(np.maximum(src, 0)), axis=2), 0.0).astype(bf16)
    pad96 = jnp.zeros((depth, 1, 96), f32)
    alog_row = jnp.concatenate([gdn_a_log.reshape(depth, 1, 16), jnp.zeros((depth, 1, 16), f32), pad96], axis=2)
    dtb_row = jnp.concatenate([gdn_dt_bias.reshape(depth, 1, 16), jnp.zeros((depth, 1, 16), f32), pad96], axis=2)
    wg = jnp.einsum('lrgnde,nm->lrndgme', lru_w_gate, jnp.eye(8, dtype=f32)).reshape(depth, 2, 512, 1024)
    return dict(
        norm1=norm1_g.reshape(depth, 1, D_MODEL),
        norm2=norm2_g.reshape(depth, 1, D_MODEL),
        w_p=w_p,
        w_out=w_out.astype(bf16),
        gdn_conv=gdn_conv,
        alog_row=alog_row,
        dtb_row=dtb_row,
        gdn_ng=jnp.tile(gdn_norm_g, (1, GDN_HEADS)).reshape(depth, 1, 512),
        gdn_consts=_gdn_consts(),
        sink=swa_sink,
        lru_conv=lru_conv,
        lru_conv_b=lru_conv_b.reshape(depth, 1, 512),
        lru_wg=wg.astype(bf16),
        lru_bg=lru_b_gate.reshape(depth, 2, 1, 1024),
        lru_lam=lru_lambda.reshape(depth, 2, 1, 512),
        diff_lambda=diff_lambda,
        diff_ng=diff_norm_g.reshape(depth, 1, 128),
        lam_init=jnp.asarray([0.8 - 0.6 * math.exp(-0.3 * i) for i in range(depth)], f32),
        w1=mlp_w1.astype(bf16),
        w2=mlp_w2.astype(bf16),
    )


def _layer(l, x, mod, row0, p):
    proj, proj16 = _norm_proj(l, x, p['norm1'], mod, row0, p['w_p'])
    gdn_args = (l, proj, p['gdn_conv'], p['alog_row'], p['dtb_row'], p['gdn_ng'], p['gdn_consts'])
    o_rev = _gdn_pass(*gdn_args, True, None)
    o_a = _gdn_pass(*gdn_args, False, o_rev)
    o_b = _swa(l, proj16, p['sink'])
    lru_args = (l, proj, p['lru_conv'], p['lru_conv_b'], p['lru_wg'], p['lru_bg'], p['lru_lam'])
    h_rev = _lru_pass(*lru_args, True, None)
    o_c = _lru_pass(*lru_args, False, h_rev)
    o_d = _diff(l, proj16, p['lam_init'], p['diff_lambda'], p['diff_ng'])
    x = _outproj(l, x, o_a, o_b, o_c, o_d, mod, row0, p['w_out'])
    return _mlp(l, x, p['norm2'], mod, row0, p['w1'], p['w2'])


def kernel(x_prompt, x_sample, c_prompt, c_sample, norm1_g, norm2_g, w_mod, b_mod, w_in, w_out, gdn_conv, gdn_a_log, gdn_dt_bias, gdn_norm_g, swa_sink, lru_conv, lru_conv_b, lru_w_gate, lru_b_gate, lru_lambda, diff_lambda, diff_norm_g, mlp_w1, mlp_w2, final_g):
    depth = w_in.shape[0]
    bp, bs = x_prompt.shape[0], x_sample.shape[0]
    assert bp + bs <= MOD_ROWS
    c_all = jnp.concatenate([c_prompt, c_sample, jnp.zeros((MOD_ROWS - bp - bs, D_MODEL), f32)], axis=0)
    mod = _modulation(c_all, w_mod, b_mod).reshape(depth * MOD_ROWS * 6, 1, D_MODEL)
    p = _prepare(norm1_g, norm2_g, w_in, w_out, gdn_conv, gdn_a_log, gdn_dt_bias, gdn_norm_g, swa_sink,
                 lru_conv, lru_conv_b, lru_w_gate, lru_b_gate, lru_lambda, diff_lambda, diff_norm_g,
                 mlp_w1, mlp_w2)

    def body(i, xs):
        l = jnp.full((1,), i, jnp.int32)
        return (_layer(l, xs[0], mod, 0, p), _layer(l, xs[1], mod, bp, p))

    xp, xs = lax.fori_loop(0, depth, body, (x_prompt, x_sample))
    return (_final_norm(xp, final_g), _final_norm(xs, final_g))
```

```python
import functools
import math

import numpy as np
import jax
import jax.numpy as jnp
from jax import lax
from jax.experimental import pallas as pl
from jax.experimental.pallas import tpu as pltpu

f32 = jnp.float32
bf16 = jnp.bfloat16

D_MODEL = 2048
GROUP_W = 512
HEAD_DIM = 64
EPS = 1e-6
GDN_HEADS = 8
GDN_CHUNK = 64
GDN_BLOCK_CHUNKS = 4
SWA_HEADS = 8
SWA_BLOCK = 128
LRU_C = 8.0
DIFF_HEADS = 4
DIFF_TQ = 256
DIFF_TK = 1024
DIFF_VROWS = 144
D_FF = 4 * D_MODEL
N_MOD = 6 * D_MODEL
MOD_ROWS = 16

OFF_AQKV, OFF_AZ, OFF_CX, OFF_CG, OFF_AAB = 0, 1536, 2048, 2560, 3072
N_PROJ32 = 3200
OFF_BQ, OFF_DQ, OFF_DK, OFF_DV, OFF_BK, OFF_BV = 0, 512, 1024, 1536, 2048, 2304
N_PROJ16 = 2560
N_PROJ = N_PROJ32 + N_PROJ16
PROJ_TN = 640
NEG = -1e30

VMEM_LIMIT = 56 * 1024 * 1024


def _proj_columns():
    src = np.full((N_PROJ,), -1, np.int64)
    src[0:2048] = np.arange(0, 2048)
    src[OFF_CX:OFF_CX + 512] = np.arange(2848, 3360)
    src[OFF_CG:OFF_CG + 512] = np.arange(3360, 3872)
    src[OFF_AAB:OFF_AAB + 32] = np.arange(2048, 2080)
    o16 = N_PROJ32
    src[o16 + OFF_BQ:o16 + OFF_BQ + 512] = np.arange(2080, 2592)
    src[o16 + OFF_DQ:o16 + OFF_DQ + 512] = np.arange(3872, 4384)
    src[o16 + OFF_DK:o16 + OFF_DK + 512] = np.arange(4384, 4896)
    src[o16 + OFF_DV:o16 + OFF_DV + 512] = np.arange(4896, 5408)
    for kv in range(2):
        for half in range(2):
            o = o16 + 128 * kv + 64 * half
            src[OFF_BK + o:OFF_BK + o + 64] = np.arange(2592 + 64 * kv, 2592 + 64 * kv + 64)
            src[OFF_BV + o:OFF_BV + o + 64] = np.arange(2720 + 64 * kv, 2720 + 64 * kv + 64)
    return src


def _cparams(sem, flags=None):
    return pltpu.CompilerParams(dimension_semantics=sem, vmem_limit_bytes=VMEM_LIMIT, flags=flags)


def _dot(a, b):
    return jnp.dot(a, b, preferred_element_type=f32)


def _dot_nt(a, b):
    return lax.dot_general(a, b, (((1,), (1,)), ((), ())), preferred_element_type=f32)


def _dot_tn(a, b):
    return lax.dot_general(a, b, (((0,), (0,)), ((), ())), preferred_element_type=f32)


def _split2(x):
    hi = x.astype(bf16)
    return hi, (x - hi.astype(f32)).astype(bf16)


def _dot_sel(x, sel):
    hi, lo = _split2(x)
    return _dot(hi, sel) + _dot(lo, sel)


def _sel_dot(sel, x):
    hi, lo = _split2(x)
    return _dot(sel, hi) + _dot(sel, lo)


def _sigmoid(x):
    return 1.0 / (1.0 + jnp.exp(-x))


def _softplus(x):
    return jnp.maximum(x, 0.0) + jnp.log(1.0 + jnp.exp(-jnp.abs(x)))


def _silu(x):
    return x * _sigmoid(x)


def _gelu_tanh(x):
    return 0.5 * x * (1.0 + jnp.tanh(math.sqrt(2.0 / math.pi) * (x + 0.044715 * (x * x * x))))


def _norm_mod(x, g, sc, sh):
    var = jnp.mean(x * x, axis=-1, keepdims=True)
    return (x * lax.rsqrt(var + EPS) * g) * (1.0 + sc) + sh


def _conv4(x, prev, nxt, w):
    T = x.shape[0]
    row = lax.broadcasted_iota(jnp.int32, x.shape, 0)
    xm2 = pltpu.roll(x, 2, 0)
    xm2 = jnp.where(row == 0, prev[6:7, :], jnp.where(row == 1, prev[7:8, :], xm2))
    xm1 = jnp.where(row == 0, prev[7:8, :], pltpu.roll(x, 1, 0))
    xp1 = jnp.where(row == T - 1, nxt[0:1, :], pltpu.roll(x, T - 1, 0))
    return w[0:1, :] * xm2 + w[1:2, :] * xm1 + w[2:3, :] * x + w[3:4, :] * xp1


def _mod_kernel(c_ref, w_ref, b_ref, o_ref):
    c = c_ref[...]
    o_ref[0] = _dot(_silu(c).astype(bf16), w_ref[0].astype(bf16)) + b_ref[0]


def _modulation(c_all, w_mod, b_mod):
    depth = w_mod.shape[0]
    tn = 1024
    return pl.pallas_call(
        _mod_kernel,
        grid=(depth, N_MOD // tn),
        in_specs=[
            pl.BlockSpec((MOD_ROWS, D_MODEL), lambda l, j: (0, 0)),
            pl.BlockSpec((1, D_MODEL, tn), lambda l, j: (l, 0, j)),
            pl.BlockSpec((1, 1, tn), lambda l, j: (l, 0, j)),
        ],
        out_specs=pl.BlockSpec((1, MOD_ROWS, tn), lambda l, j: (l, 0, j)),
        out_shape=jax.ShapeDtypeStruct((depth, MOD_ROWS, N_MOD), f32),
        compiler_params=_cparams(("arbitrary", "arbitrary")),
        name="modulation",
    )(c_all, w_mod, b_mod.reshape(depth, 1, N_MOD))


def _mod_spec(row0, chunk, ngrid):
    def imap(*a):
        b, l_ref = a[0], a[ngrid]
        return (l_ref[0] * (MOD_ROWS * 6) + (row0 + b) * 6 + chunk, 0, 0)
    return pl.BlockSpec((None, 1, D_MODEL), imap)


N_TILES32 = N_PROJ32 // PROJ_TN


def _norm_proj_kernel(l_ref, x_ref, g_ref, sc_ref, sh_ref, w_ref, o32_ref, o16_ref, h_ref):
    j = pl.program_id(2)

    @pl.when(j == 0)
    def _():
        h_ref[...] = _norm_mod(x_ref[...], g_ref[...], sc_ref[...], sh_ref[...]).astype(bf16)

    r = _dot(h_ref[...], w_ref[...])

    @pl.when(j < N_TILES32)
    def _():
        o32_ref[...] = r

    @pl.when(j >= N_TILES32)
    def _():
        o16_ref[...] = r.astype(bf16)


def _norm_proj(l, x, norm_g, mod, row0, w_p):
    B, S, _ = x.shape
    tm = min(1024, S)
    grid_spec = pltpu.PrefetchScalarGridSpec(
        num_scalar_prefetch=1,
        grid=(B, S // tm, N_PROJ // PROJ_TN),
        in_specs=[
            pl.BlockSpec((None, tm, D_MODEL), lambda b, i, j, l_ref: (b, i, 0)),
            pl.BlockSpec((None, 1, D_MODEL), lambda b, i, j, l_ref: (l_ref[0], 0, 0)),
            _mod_spec(row0, 1, 3),
            _mod_spec(row0, 0, 3),
            pl.BlockSpec((None, D_MODEL, PROJ_TN), lambda b, i, j, l_ref: (l_ref[0], 0, j)),
        ],
        out_specs=[
            pl.BlockSpec((None, tm, PROJ_TN), lambda b, i, j, l_ref: (b, i, jnp.minimum(j, N_TILES32 - 1))),
            pl.BlockSpec((None, tm, PROJ_TN), lambda b, i, j, l_ref: (b, i, jnp.maximum(j - N_TILES32, 0))),
        ],
        scratch_shapes=[pltpu.VMEM((tm, D_MODEL), bf16)],
    )
    return pl.pallas_call(
        _norm_proj_kernel,
        grid_spec=grid_spec,
        out_shape=[jax.ShapeDtypeStruct((B, S, N_PROJ32), f32), jax.ShapeDtypeStruct((B, S, N_PROJ16), bf16)],
        compiler_params=_cparams(("arbitrary", "arbitrary", "arbitrary")),
        name="norm_proj",
    )(l, x, norm_g, mod, mod, w_p)


def _gdn_kernel(l_ref, x_ref, xp_ref, xn_ref, ab_ref, *rest, rev, final):
    if final:
        (z_ref, ob_ref, cw_ref, alog_ref, dtb_ref, ng_ref, ehead_ref, expg_ref, expb_ref,
         o_ref, s_ref) = rest
    else:
        (cw_ref, alog_ref, dtb_ref, ehead_ref, expg_ref, expb_ref, o_ref, s_ref) = rest
    C = GDN_CHUNK
    TB = x_ref.shape[0]
    NC = TB // C
    i = pl.program_id(1)
    nb = pl.num_programs(1)
    blk = nb - 1 - i if rev else i

    @pl.when(i == 0)
    def _():
        s_ref[...] = jnp.zeros_like(s_ref)

    prev = jnp.where(blk > 0, xp_ref[...], 0.0)
    nxt = jnp.where(blk < nb - 1, xn_ref[...], 0.0)
    y = _silu(_conv4(x_ref[...], prev, nxt, cw_ref[...]))
    q, k, v = y[:, 0:512], y[:, 512:1024], y[:, 1024:1536]
    ehead = ehead_ref[...]
    qn = q * lax.rsqrt(_dot((q * q).astype(bf16), ehead) + EPS) * (HEAD_DIM ** -0.5)
    kn = k * lax.rsqrt(_dot((k * k).astype(bf16), ehead) + EPS)

    gin = ab_ref[...]
    g = -jnp.exp(alog_ref[...]) * _softplus(gin + dtb_ref[...])
    rt = lax.broadcasted_iota(jnp.int32, (TB, TB), 0)
    ct = lax.broadcasted_iota(jnp.int32, (TB, TB), 1)
    same = (rt // C) == (ct // C)
    ones_bd = jnp.where(same, 1.0, 0.0).astype(bf16)
    cum_bd = jnp.where(same & ((ct >= rt) if rev else (ct <= rt)), 1.0, 0.0).astype(bf16)
    gc = _sel_dot(cum_bd, g)
    gl = _sel_dot(ones_bd, g)
    expg = expg_ref[...]
    gc_x = _dot_sel(gc, expg)
    gl_x = _dot_sel(gl, expg)
    beta_x = _sigmoid(_dot_sel(gin, expb_ref[...]))
    egc_x = jnp.exp(gc_x)
    etail_x = jnp.exp(gl_x - gc_x)
    egl_x = jnp.exp(gl_x)

    kb = kn * beta_x
    vb = v * beta_x
    kw = kb * egc_x
    qh = qn * egc_x
    kt = kn * etail_x

    W = 4 * C
    r2 = lax.broadcasted_iota(jnp.int32, (W, W), 0) // C
    c2 = lax.broadcasted_iota(jnp.int32, (W, W), 1) // C
    bd = r2 == c2
    rs = lax.broadcasted_iota(jnp.int32, (C, W), 0)
    js = lax.broadcasted_iota(jnp.int32, (C, W), 1) % C
    tri = (js >= rs) if rev else (js <= rs)
    strict = (js > rs) if rev else (js < rs)
    eye_s = jnp.where(js == rs, 1.0, 0.0)
    ones_cc = jnp.ones((C, C), bf16)

    def blockdiag(m):
        mb = m.astype(bf16)
        return jnp.where(bd, jnp.concatenate([mb, mb, mb, mb], axis=0), jnp.zeros((), bf16))

    def _dot_hi(x, y):
        xh = x.astype(bf16)
        xl = (x - xh.astype(f32)).astype(bf16)
        yh = y.astype(bf16)
        bh = blockdiag(yh)
        return _dot(xh, bh) + _dot(xl, bh) + _dot(xh, blockdiag(y - yh.astype(f32)))

    pairs = [(c, grp) for c in range(NC) for grp in range(2)]

    def part(x, pr):
        c, grp = pr
        return x[C * c:C * c + C, W * grp:W * grp + W]

    a_mat, intra_w, t_inv = {}, {}, {}
    for pr in pairs:
        kst = blockdiag(part(kn, pr))
        kk = _dot_nt(part(kb, pr).astype(bf16), kst)
        qk = _dot_nt(part(qn, pr).astype(bf16), kst)
        col = part(gc_x, pr)
        rowv = _sel_dot(ones_cc, col * eye_s)
        dm = jnp.exp(jnp.where(tri, col - rowv, NEG))
        a_mat[pr] = jnp.where(strict, kk * dm, 0.0)
        intra_w[pr] = (qk * dm).astype(bf16)
        t_inv[pr] = eye_s - jnp.where(rs // 2 == js // 2, a_mat[pr], 0.0)
    s = 2
    while s < C:
        off = (rs // (2 * s) == js // (2 * s)) & (rs // s != js // s)
        ys = {pr: _dot_hi(t_inv[pr], jnp.where(off, a_mat[pr], 0.0)) for pr in pairs}
        t_inv = {pr: t_inv[pr] - _dot_hi(ys[pr], t_inv[pr]) for pr in pairs}
        s *= 2
    pre = {}
    for pr in pairs:
        tb = t_inv[pr].astype(bf16)
        pre[pr] = (_dot(tb, blockdiag(part(vb, pr))), _dot(tb, blockdiag(part(kw, pr))), intra_w[pr])

    rows = [None] * NC
    order = range(NC - 1, -1, -1) if rev else range(NC)
    sts = [s_ref[0], s_ref[1]]
    for c in order:
        rw = slice(C * c, C * c + C)
        outs = []
        for grp in range(2):
            sl = slice(W * grp, W * grp + W)
            u, w, intra = pre[(c, grp)]
            st = sts[grp]
            wq = jnp.concatenate([w, qh[rw, sl]], axis=0).astype(bf16)
            ws = _dot(wq, st.astype(bf16))
            vnew = u - ws[0:C]
            outs.append(ws[C:2 * C] + _dot(intra, blockdiag(vnew)))
            upd = _dot_tn(kt[rw, sl].astype(bf16), vnew.astype(bf16))
            sts[grp] = st * egl_x[C * c:C * c + 1, sl] + jnp.where(bd, upd, 0.0)
        rows[c] = jnp.concatenate(outs, axis=1)
    s_ref[0] = sts[0]
    s_ref[1] = sts[1]
    o_full = jnp.concatenate(rows, axis=0)
    if final:
        tot = o_full + ob_ref[...]
        ms = _dot((tot * tot).astype(bf16), ehead) * (1.0 / HEAD_DIM)
        o_ref[...] = (tot * lax.rsqrt(ms + EPS) * ng_ref[...] * _silu(z_ref[...])).astype(o_ref.dtype)
    else:
        o_ref[...] = o_full


def _gdn_consts():
    lane = np.arange(512)
    ehead = (lane[:, None] // HEAD_DIM == lane[None, :] // HEAD_DIM).astype(np.float32)
    expg = np.zeros((2, 128, 512), np.float32)
    expb = np.zeros((2, 128, 512), np.float32)
    for r in range(2):
        expg[r, 8 * r + lane // HEAD_DIM, lane] = 1.0
        expb[r, 16 + 8 * r + lane // HEAD_DIM, lane] = 1.0
    return jnp.asarray(ehead, bf16), jnp.asarray(expg, bf16), jnp.asarray(expb, bf16)


def _gdn_pass(l, proj, conv_w, alog_row, dtb_row, ng_row, consts, rev, ob):
    B, S, _ = proj.shape
    TB = min(GDN_BLOCK_CHUNKS * GDN_CHUNK, S)
    nb = S // TB
    n8 = S // 8
    final = not rev
    ehead, expg, expb = consts

    def pos(i):
        return nb - 1 - i if rev else i

    in_specs = [
        pl.BlockSpec((None, TB, 1536), lambda b, i, l_ref: (b, pos(i), 0)),
        pl.BlockSpec((None, 8, 1536), lambda b, i, l_ref: (b, jnp.maximum(pos(i) * (TB // 8) - 1, 0), 0)),
        pl.BlockSpec((None, 8, 1536), lambda b, i, l_ref: (b, jnp.minimum((pos(i) + 1) * (TB // 8), n8 - 1), 0)),
        pl.BlockSpec((None, TB, 128), lambda b, i, l_ref: (b, pos(i), OFF_AAB // 128)),
    ]
    args = [proj, proj, proj, proj]
    if final:
        in_specs += [
            pl.BlockSpec((None, TB, 512), lambda b, i, l_ref: (b, pos(i), OFF_AZ // 512)),
            pl.BlockSpec((None, TB, 512), lambda b, i, l_ref: (b, pos(i), 0)),
        ]
        args += [proj, ob]
    in_specs += [
        pl.BlockSpec((None, 4, 1536), lambda b, i, l_ref: (l_ref[0], 0, 0)),
        pl.BlockSpec((None, 1, 128), lambda b, i, l_ref: (l_ref[0], 0, 0)),
        pl.BlockSpec((None, 1, 128), lambda b, i, l_ref: (l_ref[0], 0, 0)),
    ]
    args += [conv_w, alog_row, dtb_row]
    if final:
        in_specs.append(pl.BlockSpec((None, 1, 512), lambda b, i, l_ref: (l_ref[0], 0, 0)))
        args.append(ng_row)
    in_specs += [
        pl.BlockSpec((512, 512), lambda b, i, l_ref: (0, 0)),
        pl.BlockSpec((None, 128, 512), lambda b, i, l_ref: (1 if rev else 0, 0, 0)),
        pl.BlockSpec((None, 128, 512), lambda b, i, l_ref: (1 if rev else 0, 0, 0)),
    ]
    args += [ehead, expg, expb]
    grid_spec = pltpu.PrefetchScalarGridSpec(
        num_scalar_prefetch=1,
        grid=(B, nb),
        in_specs=in_specs,
        out_specs=pl.BlockSpec((None, TB, 512), lambda b, i, l_ref: (b, pos(i), 0)),
        scratch_shapes=[pltpu.VMEM((2, 4 * GDN_CHUNK, 4 * GDN_CHUNK), f32)],
    )
    return pl.pallas_call(
        functools.partial(_gdn_kernel, rev=rev, final=final),
        grid_spec=grid_spec,
        out_shape=jax.ShapeDtypeStruct((B, S, 512), bf16 if final else f32),
        compiler_params=_cparams(("arbitrary", "arbitrary")),
        name="gdn_fwd" if final else "gdn_bwd",
    )(l, *args)


def _swa_kernel(l_ref, sink_ref, q_ref, kp_ref, kc_ref, kn_ref, vp_ref, vc_ref, vn_ref, o_ref):
    T = SWA_BLOCK
    i = pl.program_id(1)
    nb = pl.num_programs(1)
    l = l_ref[0]
    row = lax.broadcasted_iota(jnp.int32, (2 * T, 3 * T), 0)
    col = lax.broadcasted_iota(jnp.int32, (2 * T, 3 * T), 1)
    top = row < T
    r = jnp.where(top, row, row - T)
    disti = jnp.abs(r + T - col)
    dist = disti.astype(f32)
    valid = (disti <= SWA_BLOCK) & ((col >= T) | (i > 0)) & ((col < 2 * T) | (i < nb - 1))
    lo = lax.broadcasted_iota(jnp.int32, (T, 128), 1) < HEAD_DIM
    top1 = lax.broadcasted_iota(jnp.int32, (2 * T, 1), 0) < T
    for kv in range(2):
        ksl = slice(128 * kv, 128 * kv + 128)
        kd = jnp.concatenate([kp_ref[:, ksl], kc_ref[:, ksl], kn_ref[:, ksl]], axis=0)
        vd = jnp.concatenate([vp_ref[:, ksl], vc_ref[:, ksl], vn_ref[:, ksl]], axis=0)
        for pp in range(2):
            p = 2 * kv + pp
            qp = q_ref[:, 128 * p:128 * p + 128].astype(f32) * (HEAD_DIM ** -0.5)
            lhs = jnp.concatenate([jnp.where(lo, qp, 0.0), jnp.where(lo, 0.0, qp)], axis=0).astype(bf16)
            s = _dot_nt(lhs, kd)
            slope = jnp.where(top, 2.0 ** -(2 * p + 1), 2.0 ** -(2 * p + 2))
            s = jnp.where(valid, s - slope * dist, NEG)
            sk = jnp.where(top1, sink_ref[l, 2 * p], sink_ref[l, 2 * p + 1])
            m = jnp.maximum(jnp.max(s, axis=1, keepdims=True), sk)
            e = jnp.exp(s - m)
            den = jnp.sum(e, axis=1, keepdims=True) + jnp.exp(sk - m)
            o = _dot(e.astype(bf16), vd) / den
            o_ref[:, 128 * p:128 * p + 128] = jnp.where(lo, o[0:T], o[T:2 * T]).astype(o_ref.dtype)


def _swa(l, proj16, sink):
    B, S, _ = proj16.shape
    T = SWA_BLOCK
    nb = S // T
    kb, vb = OFF_BK // 256, OFF_BV // 256

    def side(blk, which):
        if which < 0:
            return lambda b, i, l_ref: (b, jnp.maximum(i - 1, 0), blk)
        if which > 0:
            return lambda b, i, l_ref: (b, jnp.minimum(i + 1, nb - 1), blk)
        return lambda b, i, l_ref: (b, i, blk)

    grid_spec = pltpu.PrefetchScalarGridSpec(
        num_scalar_prefetch=1,
        grid=(B, nb),
        in_specs=[
            pl.BlockSpec(memory_space=pltpu.SMEM),
            pl.BlockSpec((None, T, 512), lambda b, i, l_ref: (b, i, OFF_BQ // 512)),
            pl.BlockSpec((None, T, 256), side(kb, -1)),
            pl.BlockSpec((None, T, 256), side(kb, 0)),
            pl.BlockSpec((None, T, 256), side(kb, 1)),
            pl.BlockSpec((None, T, 256), side(vb, -1)),
            pl.BlockSpec((None, T, 256), side(vb, 0)),
            pl.BlockSpec((None, T, 256), side(vb, 1)),
        ],
        out_specs=pl.BlockSpec((None, T, 512), lambda b, i, l_ref: (b, i, 0)),
    )
    return pl.pallas_call(
        _swa_kernel,
        grid_spec=grid_spec,
        out_shape=jax.ShapeDtypeStruct((B, S, 512), bf16),
        compiler_params=_cparams(("arbitrary", "arbitrary")),
        name="swa",
    )(l, sink, proj16, proj16, proj16, proj16, proj16, proj16, proj16)


def _lru_kernel(l_ref, x_ref, xp_ref, xn_ref, *rest, rev, final):
    if final:
        (gate_ref, hb_ref, cw_ref, cb_ref, wg_ref, bg_ref, lam_ref, o_ref, carry_ref) = rest
    else:
        (cw_ref, cb_ref, wg_ref, bg_ref, lam_ref, o_ref, carry_ref) = rest
    T = x_ref.shape[0]
    i = pl.program_id(1)
    nb = pl.num_programs(1)
    blk = nb - 1 - i if rev else i

    @pl.when(i == 0)
    def _():
        carry_ref[...] = jnp.zeros_like(carry_ref)

    prev = jnp.where(blk > 0, xp_ref[...], 0.0)
    nxt = jnp.where(blk < nb - 1, xn_ref[...], 0.0)
    xc = _conv4(x_ref[...], prev, nxt, cw_ref[...]) + cb_ref[...]
    gates = _sigmoid(_dot(xc.astype(bf16), wg_ref[...]) + bg_ref[...])
    rg, ig = gates[:, 0:512], gates[:, 512:1024]
    log_a = (-LRU_C) * rg * _softplus(-lam_ref[...])
    a = jnp.exp(log_a)
    y2 = 2.0 * log_a
    one_minus_a2 = -jnp.tanh(0.5 * y2) * (jnp.exp(y2) + 1.0)
    bv = jnp.sqrt(one_minus_a2) * ig * xc
    row = lax.broadcasted_iota(jnp.int32, (T, 512), 0)
    s = 1
    while s < T:
        if rev:
            ok = row < T - s
            a_s, b_s = pltpu.roll(a, T - s, 0), pltpu.roll(bv, T - s, 0)
        else:
            ok = row >= s
            a_s, b_s = pltpu.roll(a, s, 0), pltpu.roll(bv, s, 0)
        bv = jnp.where(ok, a * b_s + bv, bv)
        a = jnp.where(ok, a * a_s, a)
        s *= 2
    h = a * carry_ref[...] + bv
    carry_ref[...] = h[0:1, :] if rev else h[T - 1:T, :]
    if final:
        o_ref[...] = ((h + hb_ref[...]) * _gelu_tanh(gate_ref[...])).astype(o_ref.dtype)
    else:
        o_ref[...] = h


def _lru_pass(l, proj, conv_w, conv_b, wg, bg, lam, rev, hb):
    B, S, _ = proj.shape
    T = min(256, S)
    nb = S // T
    n8 = S // 8
    final = not rev
    d = 1 if rev else 0
    xb = OFF_CX // 512

    def pos(i):
        return nb - 1 - i if rev else i

    in_specs = [
        pl.BlockSpec((None, T, 512), lambda b, i, l_ref: (b, pos(i), xb)),
        pl.BlockSpec((None, 8, 512), lambda b, i, l_ref: (b, jnp.maximum(pos(i) * (T // 8) - 1, 0), xb)),
        pl.BlockSpec((None, 8, 512), lambda b, i, l_ref: (b, jnp.minimum((pos(i) + 1) * (T // 8), n8 - 1), xb)),
    ]
    args = [proj, proj, proj]
    if final:
        in_specs += [
            pl.BlockSpec((None, T, 512), lambda b, i, l_ref: (b, pos(i), OFF_CG // 512)),
            pl.BlockSpec((None, T, 512), lambda b, i, l_ref: (b, pos(i), 0)),
        ]
        args += [proj, hb]
    in_specs += [
        pl.BlockSpec((None, 4, 512), lambda b, i, l_ref: (l_ref[0], 0, 0)),
        pl.BlockSpec((None, 1, 512), lambda b, i, l_ref: (l_ref[0], 0, 0)),
        pl.BlockSpec((None, None, 512, 1024), lambda b, i, l_ref: (l_ref[0], d, 0, 0)),
        pl.BlockSpec((None, None, 1, 1024), lambda b, i, l_ref: (l_ref[0], d, 0, 0)),
        pl.BlockSpec((None, None, 1, 512), lambda b, i, l_ref: (l_ref[0], d, 0, 0)),
    ]
    args += [conv_w, conv_b, wg, bg, lam]
    grid_spec = pltpu.PrefetchScalarGridSpec(
        num_scalar_prefetch=1,
        grid=(B, nb),
        in_specs=in_specs,
        out_specs=pl.BlockSpec((None, T, 512), lambda b, i, l_ref: (b, pos(i), 0)),
        scratch_shapes=[pltpu.VMEM((1, 512), f32)],
    )
    return pl.pallas_call(
        functools.partial(_lru_kernel, rev=rev, final=final),
        grid_spec=grid_spec,
        out_shape=jax.ShapeDtypeStruct((B, S, 512), bf16 if final else f32),
        compiler_params=_cparams(("arbitrary", "arbitrary")),
        name="lru_fwd" if final else "lru_bwd",
    )(l, *args)


def _diff_feat_consts(tk):
    c = np.arange(tk)
    feat = np.zeros((tk, 128), np.float32)
    feat[:, 0] = 1.0
    feat[:, 1] = c % 256
    feat[:, 2] = c // 256
    return jnp.asarray(feat, bf16)


def _diff_kernel(l_ref, laminit_ref, q_ref, k_ref, vt_ref, feat_ref, lv_ref, g_ref, o_ref,
                 lhs_ref, m_ref, alpha_ref, acc_ref, sbuf0_ref, sbuf1_ref, pbuf0_ref, pbuf1_ref, kn2_ref):
    sbufs = (sbuf0_ref, sbuf1_ref)
    pbufs = (pbuf0_ref, pbuf1_ref)
    tq = q_ref.shape[0]
    nk, _, tk = vt_ref.shape
    assert nk % 2 == 0
    RB = 64
    h = pl.program_id(1)
    qi = pl.program_id(2)
    q0 = qi * tq
    slope = jnp.where(h == 0, 2.0 ** -2, jnp.where(h == 1, 2.0 ** -4, jnp.where(h == 2, 2.0 ** -6, 2.0 ** -8)))

    qp = q_ref[...].astype(f32) * (HEAD_DIM ** -0.5)
    lane = lax.broadcasted_iota(jnp.int32, (2 * tq, 128), 1)
    rr = lax.broadcasted_iota(jnp.int32, (2 * tq, 128), 0)
    rr = jnp.where(rr < tq, rr, rr - tq).astype(f32)
    lo = lax.broadcasted_iota(jnp.int32, (tq, 128), 1) < HEAD_DIM
    qm = jnp.concatenate([jnp.where(lo, qp, 0.0), jnp.where(lo, 0.0, qp)], axis=0).astype(bf16)
    fplus = jnp.where(lane == 0, -slope * rr,
                      jnp.where(lane == 1, slope, jnp.where(lane == 2, 256.0 * slope, 0.0)))
    lhs_ref[0] = jnp.concatenate([qm, fplus.astype(bf16)], axis=1)
    lhs_ref[1] = jnp.concatenate([qm, (-fplus).astype(bf16)], axis=1)
    lhs_ref[2] = jnp.concatenate([qm, jnp.zeros((2 * tq, 128), bf16)], axis=1)
    m_ref[...] = jnp.full_like(m_ref, NEG)
    alpha_ref[...] = jnp.ones_like(alpha_ref)
    acc_ref[...] = jnp.zeros_like(acc_ref)
    pbuf1_ref[...] = jnp.zeros_like(pbuf1_ref)

    def variant(ki):
        d = q0 - ki * tk
        return d, jnp.where(d >= tk, 0, jnp.where(d <= -tq, 1, 2))

    def scores(ki, slot):
        ki = jnp.minimum(ki, nk - 1)
        _, var = variant(ki)
        kx = jnp.concatenate([k_ref[pl.ds(pl.multiple_of(ki * tk, tk), tk), :], feat_ref[...]], axis=1)
        sbufs[slot][...] = _dot_nt(kx, lhs_ref[var])

    def accumulate(ki, slot):
        acc_ref[...] = alpha_ref[slot] * acc_ref[...] + _dot(vt_ref[ki], pbufs[slot][...])

    def softmax(ki, slot):
        d, var = variant(ki)
        mx = jnp.full((8, 2 * tq), NEG, f32)
        for rb in range(tk // RB):
            sb = sbufs[slot][rb * RB:rb * RB + RB, :]
            for r8 in range(RB // 8):
                mx = jnp.maximum(mx, sb[8 * r8:8 * r8 + 8, :])
        dabs = jnp.abs(jnp.zeros((1, 2 * tq), jnp.int32) + d).astype(f32)
        cd = jnp.where(var == 2, 0.0, slope * dabs)
        m_prev = m_ref[...]
        m_new = jnp.maximum(m_prev, jnp.max(mx, axis=0, keepdims=True) - cd)
        alpha_ref[slot] = jnp.exp(m_prev - m_new)
        m_ref[...] = m_new
        shift = m_new + cd
        for rb in range(tk // RB):
            rows = slice(rb * RB, rb * RB + RB)
            pbufs[slot][rows, :] = jnp.exp(sbufs[slot][rows, :] - shift).astype(bf16)

    def diag_bias(ki, slot):
        d, var = variant(ki)

        @pl.when(var == 2)
        def _():
            for rb in range(tk // RB):
                rows = slice(rb * RB, rb * RB + RB)
                ci = lax.broadcasted_iota(jnp.int32, (RB, 2 * tq), 0) + (rb * RB)
                ji = lax.broadcasted_iota(jnp.int32, (RB, 2 * tq), 1)
                r = jnp.where(ji < tq, ji, ji - tq)
                sbufs[slot][rows, :] = sbufs[slot][rows, :] - slope * jnp.abs(r - ci + d).astype(f32)

    @pl.when(qi == 0)
    def _():
        def knorm(i, mx):
            kk = k_ref[pl.ds(pl.multiple_of(i * tq, tq), tq), :].astype(f32)
            sq = _dot((kk * kk).astype(bf16), jnp.ones((128, 128), bf16))
            for r8 in range(tq // 8):
                mx = jnp.maximum(mx, sq[8 * r8:8 * r8 + 8, :])
            return mx
        mx = lax.fori_loop(0, (nk * tk) // tq, knorm, jnp.zeros((8, 128), f32))
        kn2_ref[0] = jnp.max(mx)

    qf = q_ref[...].astype(f32)
    qn2 = jnp.max(_dot((qf * qf).astype(bf16), jnp.ones((128, 128), bf16)), keepdims=True)
    krows = k_ref[pl.ds(pl.multiple_of(q0, tq), tq), :].astype(f32)
    lane_m = lax.broadcasted_iota(jnp.int32, (128, 128), 0) // HEAD_DIM
    lane_n = lax.broadcasted_iota(jnp.int32, (128, 128), 1) // HEAD_DIM
    same_map = jnp.where(lane_m == lane_n, 1.0, 0.0).astype(bf16)
    self_t = _dot((qf * krows).astype(bf16), same_map) * (HEAD_DIM ** -0.5)
    m_low = jnp.min(self_t, keepdims=True)
    qk_hi = jnp.sqrt(qn2 * kn2_ref[0]) * (HEAD_DIM ** -0.5)
    dist = (1.02 * qk_hi + 1.0 - m_low + 105.0) / slope
    q0f = (jnp.zeros((1, 1), jnp.int32) + q0).astype(f32)
    first = jnp.floor((q0f - (tk - 1) - dist) / tk) + 1.0
    last = jnp.ceil((dist + q0f + (tq - 1)) / tk)
    first = jnp.max(jnp.clip(first, 0.0, float(nk)).astype(jnp.int32))
    last = jnp.max(jnp.clip(last, 0.0, float(nk)).astype(jnp.int32))
    p_first = first // 2
    p_last = (last + 1) // 2

    scores(2 * p_first, 0)

    def body(j, carry):
        ka = 2 * j
        diag_bias(ka, 0)
        softmax(ka, 0)
        scores(ka + 1, 1)
        accumulate(jnp.maximum(ka - 1, 0), 1)
        diag_bias(ka + 1, 1)
        softmax(ka + 1, 1)
        scores(ka + 2, 0)
        accumulate(ka, 0)
        return carry

    lax.fori_loop(p_first, p_last, body, 0)
    accumulate(2 * p_last - 1, 1)

    lam_init = laminit_ref[l_ref[0]]
    lv = lv_ref[...]
    lam = (jnp.exp(jnp.sum(lv[0:1] * lv[1:2], axis=1, keepdims=True))
           - jnp.exp(jnp.sum(lv[2:3] * lv[3:4], axis=1, keepdims=True)) + lam_init)
    acc = acc_ref[...]
    ot = acc[0:128, :] / acc[128:129, :]
    od = (ot[:, 0:tq] - lam * ot[:, tq:2 * tq]).T
    var = jnp.mean(od * od, axis=-1, keepdims=True)
    o_ref[...] = (od * lax.rsqrt(var + EPS) * g_ref[...] * (1.0 - lam_init)).astype(o_ref.dtype)


def _diff(l, proj16, lam_init_tab, diff_lambda, norm_g):
    B, S, _ = proj16.shape
    tq = min(DIFF_TQ, S)
    tk = min(DIFF_TK, S // 4)
    nk = S // tk
    vt = proj16[:, :, OFF_DV:OFF_DV + 512].reshape(B, nk, tk, DIFF_HEADS, 128).transpose(0, 3, 1, 4, 2)
    extra = jnp.zeros((B, DIFF_HEADS, nk, DIFF_VROWS - 128, tk), bf16).at[:, :, :, 0, :].set(1.0)
    vt = jnp.concatenate([vt, extra], axis=3)
    grid_spec = pltpu.PrefetchScalarGridSpec(
        num_scalar_prefetch=1,
        grid=(B, DIFF_HEADS, S // tq),
        in_specs=[
            pl.BlockSpec(memory_space=pltpu.SMEM),
            pl.BlockSpec((None, tq, 128), lambda b, h, qi, l_ref: (b, qi, OFF_DQ // 128 + h)),
            pl.BlockSpec((None, S, 128), lambda b, h, qi, l_ref: (b, 0, OFF_DK // 128 + h)),
            pl.BlockSpec((None, None, nk, DIFF_VROWS, tk), lambda b, h, qi, l_ref: (b, h, 0, 0, 0)),
            pl.BlockSpec((tk, 128), lambda b, h, qi, l_ref: (0, 0)),
            pl.BlockSpec((None, 4, HEAD_DIM), lambda b, h, qi, l_ref: (l_ref[0], 0, 0)),
            pl.BlockSpec((None, 1, 128), lambda b, h, qi, l_ref: (l_ref[0], 0, 0)),
        ],
        out_specs=pl.BlockSpec((None, tq, 128), lambda b, h, qi, l_ref: (b, qi, h)),
        scratch_shapes=[
            pltpu.VMEM((3, 2 * tq, 256), bf16),
            pltpu.VMEM((1, 2 * tq), f32),
            pltpu.VMEM((2, 1, 2 * tq), f32),
            pltpu.VMEM((DIFF_VROWS, 2 * tq), f32),
            pltpu.VMEM((tk, 2 * tq), f32),
            pltpu.VMEM((tk, 2 * tq), f32),
            pltpu.VMEM((tk, 2 * tq), bf16),
            pltpu.VMEM((tk, 2 * tq), bf16),
            pltpu.SMEM((1,), f32),
        ],
    )
    return pl.pallas_call(
        _diff_kernel,
        grid_spec=grid_spec,
        out_shape=jax.ShapeDtypeStruct((B, S, 512), bf16),
        compiler_params=_cparams(("arbitrary", "arbitrary", "arbitrary")),
        name="diff_attn",
    )(l, lam_init_tab, proj16, proj16, vt, _diff_feat_consts(tk), diff_lambda, norm_g)


def _outproj_kernel(l_ref, x_ref, oa_ref, ob_ref, oc_ref, od_ref, gate_ref, w_ref, o_ref):
    acc = _dot(oa_ref[...], w_ref[0:512, :])
    acc += _dot(ob_ref[...], w_ref[512:1024, :])
    acc += _dot(oc_ref[...], w_ref[1024:1536, :])
    acc += _dot(od_ref[...], w_ref[1536:2048, :])
    o_ref[...] = x_ref[...] + gate_ref[...] * acc


def _outproj(l, x, oa, ob, oc, od, mod, row0, w_out):
    B, S, _ = x.shape
    tm = min(512, S)
    mix = pl.BlockSpec((None, tm, 512), lambda b, i, l_ref: (b, i, 0))
    grid_spec = pltpu.PrefetchScalarGridSpec(
        num_scalar_prefetch=1,
        grid=(B, S // tm),
        in_specs=[
            pl.BlockSpec((None, tm, D_MODEL), lambda b, i, l_ref: (b, i, 0)),
            mix, mix, mix, mix,
            _mod_spec(row0, 2, 2),
            pl.BlockSpec((None, D_MODEL, D_MODEL), lambda b, i, l_ref: (l_ref[0], 0, 0)),
        ],
        out_specs=pl.BlockSpec((None, tm, D_MODEL), lambda b, i, l_ref: (b, i, 0)),
    )
    return pl.pallas_call(
        _outproj_kernel,
        grid_spec=grid_spec,
        out_shape=jax.ShapeDtypeStruct((B, S, D_MODEL), f32),
        compiler_params=_cparams(("arbitrary", "arbitrary")),
        name="out_proj",
    )(l, x, oa, ob, oc, od, mod, w_out)


def _mlp_kernel(l_ref, x_ref, g_ref, sc_ref, sh_ref, gate_ref, w1_ref, w2_ref, o_ref, h_ref):
    f = pl.program_id(2)

    @pl.when(f == 0)
    def _():
        h_ref[...] = _norm_mod(x_ref[...], g_ref[...], sc_ref[...], sh_ref[...]).astype(bf16)
        o_ref[...] = jnp.zeros_like(o_ref)

    t = jnp.maximum(_dot(h_ref[...], w1_ref[...]), 0.0)
    o_ref[...] += _dot((t * t).astype(bf16), w2_ref[...])

    @pl.when(f == pl.num_programs(2) - 1)
    def _():
        o_ref[...] = x_ref[...] + gate_ref[...] * o_ref[...]


def _mlp(l, x, norm_g, mod, row0, w1, w2):
    B, S, _ = x.shape
    tm = min(1024, S)
    tf = 512
    grid_spec = pltpu.PrefetchScalarGridSpec(
        num_scalar_prefetch=1,
        grid=(B, S // tm, D_FF // tf),
        in_specs=[
            pl.BlockSpec((None, tm, D_MODEL), lambda b, i, f, l_ref: (b, i, 0)),
            pl.BlockSpec((None, 1, D_MODEL), lambda b, i, f, l_ref: (l_ref[0], 0, 0)),
            _mod_spec(row0, 4, 3),
            _mod_spec(row0, 3, 3),
            _mod_spec(row0, 5, 3),
            pl.BlockSpec((None, D_MODEL, tf), lambda b, i, f, l_ref: (l_ref[0], 0, f)),
            pl.BlockSpec((None, tf, D_MODEL), lambda b, i, f, l_ref: (l_ref[0], f, 0)),
        ],
        out_specs=pl.BlockSpec((None, tm, D_MODEL), lambda b, i, f, l_ref: (b, i, 0)),
        scratch_shapes=[pltpu.VMEM((tm, D_MODEL), bf16)],
    )
    return pl.pallas_call(
        _mlp_kernel,
        grid_spec=grid_spec,
        out_shape=jax.ShapeDtypeStruct((B, S, D_MODEL), f32),
        compiler_params=_cparams(("arbitrary", "arbitrary", "arbitrary")),
        name="mlp",
    )(l, x, norm_g, mod, mod, mod, w1, w2)


def _final_norm_kernel(x_ref, g_ref, o_ref):
    x = x_ref[...]
    var = jnp.mean(x * x, axis=-1, keepdims=True)
    o_ref[...] = x * lax.rsqrt(var + EPS) * g_ref[...]


def _final_norm(x, g):
    B, S, _ = x.shape
    tm = min(1024, S)
    return pl.pallas_call(
        _final_norm_kernel,
        grid=(B, S // tm),
        in_specs=[
            pl.BlockSpec((None, tm, D_MODEL), lambda b, i: (b, i, 0)),
            pl.BlockSpec((1, D_MODEL), lambda b, i: (0, 0)),
        ],
        out_specs=pl.BlockSpec((None, tm, D_MODEL), lambda b, i: (b, i, 0)),
        out_shape=jax.ShapeDtypeStruct((B, S, D_MODEL), f32),
        compiler_params=_cparams(("arbitrary", "arbitrary")),
        name="final_norm",
    )(x, g.reshape(1, D_MODEL))


def _prepare(norm1_g, norm2_g, w_in, w_out, gdn_conv, gdn_a_log, gdn_dt_bias, gdn_norm_g, swa_sink,
             lru_conv, lru_conv_b, lru_w_gate, lru_b_gate, lru_lambda, diff_lambda, diff_norm_g,
             mlp_w1, mlp_w2):
    depth = w_in.shape[0]
    src = _proj_columns()
    w_p = jnp.where(jnp.asarray(src >= 0)[None, None, :],
                    jnp.take(w_in, jnp.asarray(np.maximum(src, 0)), axis=2), 0.0).astype(bf16)
    pad96 = jnp.zeros((depth, 1, 96), f32)
    alog_row = jnp.concatenate([gdn_a_log.reshape(depth, 1, 16), jnp.zeros((depth, 1, 16), f32), pad96], axis=2)
    dtb_row = jnp.concatenate([gdn_dt_bias.reshape(depth, 1, 16), jnp.zeros((depth, 1, 16), f32), pad96], axis=2)
    wg = jnp.einsum('lrgnde,nm->lrndgme', lru_w_gate, jnp.eye(8, dtype=f32)).reshape(depth, 2, 512, 1024)
    return dict(
        norm1=norm1_g.reshape(depth, 1, D_MODEL),
        norm2=norm2_g.reshape(depth, 1, D_MODEL),
        w_p=w_p,
        w_out=w_out.astype(bf16),
        gdn_conv=gdn_conv,
        alog_row=alog_row,
        dtb_row=dtb_row,
        gdn_ng=jnp.tile(gdn_norm_g, (1, GDN_HEADS)).reshape(depth, 1, 512),
        gdn_consts=_gdn_consts(),
        sink=swa_sink,
        lru_conv=lru_conv,
        lru_conv_b=lru_conv_b.reshape(depth, 1, 512),
        lru_wg=wg.astype(bf16),
        lru_bg=lru_b_gate.reshape(depth, 2, 1, 1024),
        lru_lam=lru_lambda.reshape(depth, 2, 1, 512),
        diff_lambda=diff_lambda,
        diff_ng=diff_norm_g.reshape(depth, 1, 128),
        lam_init=jnp.asarray([0.8 - 0.6 * math.exp(-0.3 * i) for i in range(depth)], f32),
        w1=mlp_w1.astype(bf16),
        w2=mlp_w2.astype(bf16),
    )


def _layer(l, x, mod, row0, p):
    proj, proj16 = _norm_proj(l, x, p['norm1'], mod, row0, p['w_p'])
    gdn_args = (l, proj, p['gdn_conv'], p['alog_row'], p['dtb_row'], p['gdn_ng'], p['gdn_consts'])
    o_rev = _gdn_pass(*gdn_args, True, None)
    o_a = _gdn_pass(*gdn_args, False, o_rev)
    o_b = _swa(l, proj16, p['sink'])
    lru_args = (l, proj, p['lru_conv'], p['lru_conv_b'], p['lru_wg'], p['lru_bg'], p['lru_lam'])
    h_rev = _lru_pass(*lru_args, True, None)
    o_c = _lru_pass(*lru_args, False, h_rev)
    o_d = _diff(l, proj16, p['lam_init'], p['diff_lambda'], p['diff_ng'])
    x = _outproj(l, x, o_a, o_b, o_c, o_d, mod, row0, p['w_out'])
    return _mlp(l, x, p['norm2'], mod, row0, p['w1'], p['w2'])


def kernel(x_prompt, x_sample, c_prompt, c_sample, norm1_g, norm2_g, w_mod, b_mod, w_in, w_out, gdn_conv, gdn_a_log, gdn_dt_bias, gdn_norm_g, swa_sink, lru_conv, lru_conv_b, lru_w_gate, lru_b_gate, lru_lambda, diff_lambda, diff_norm_g, mlp_w1, mlp_w2, final_g):
    depth = w_in.shape[0]
    bp, bs = x_prompt.shape[0], x_sample.shape[0]
    assert bp + bs <= MOD_ROWS
    c_all = jnp.concatenate([c_prompt, c_sample, jnp.zeros((MOD_ROWS - bp - bs, D_MODEL), f32)], axis=0)
    mod = _modulation(c_all, w_mod, b_mod).reshape(depth * MOD_ROWS * 6, 1, D_MODEL)
    p = _prepare(norm1_g, norm2_g, w_in, w_out, gdn_conv, gdn_a_log, gdn_dt_bias, gdn_norm_g, swa_sink,
                 lru_conv, lru_conv_b, lru_w_gate, lru_b_gate, lru_lambda, diff_lambda, diff_norm_g,
                 mlp_w1, mlp_w2)

    def body(i, xs):
        l = jnp.full((1,), i, jnp.int32)
        return (_layer(l, xs[0], mod, 0, p), _layer(l, xs[1], mod, bp, p))

    xp, xs = lax.fori_loop(0, depth, body, (x_prompt, x_sample))
    return (_final_norm(xp, final_g), _final_norm(xs, final_g))
```

```python
import functools
import math

import numpy as np
import jax
import jax.numpy as jnp
from jax import lax
from jax.experimental import pallas as pl
from jax.experimental.pallas import tpu as pltpu

f32 = jnp.float32
bf16 = jnp.bfloat16

D_MODEL = 2048
GROUP_W = 512
HEAD_DIM = 64
EPS = 1e-6
GDN_HEADS = 8
GDN_CHUNK = 64
GDN_BLOCK_CHUNKS = 4
SWA_HEADS = 8
SWA_BLOCK = 128
LRU_C = 8.0
DIFF_HEADS = 4
DIFF_TQ = 256
DIFF_TK = 1024
DIFF_VROWS = 144
D_FF = 4 * D_MODEL
N_MOD = 6 * D_MODEL
MOD_ROWS = 16

OFF_AQKV, OFF_AZ, OFF_CX, OFF_CG, OFF_AAB = 0, 1536, 2048, 2560, 3072
N_PROJ32 = 3200
OFF_BQ, OFF_DQ, OFF_DK, OFF_DV, OFF_BK, OFF_BV = 0, 512, 1024, 1536, 2048, 2304
N_PROJ16 = 2560
N_PROJ = N_PROJ32 + N_PROJ16
PROJ_TN = 640
NEG = -1e30

VMEM_LIMIT = 56 * 1024 * 1024


def _proj_columns():
    src = np.full((N_PROJ,), -1, np.int64)
    src[0:2048] = np.arange(0, 2048)
    src[OFF_CX:OFF_CX + 512] = np.arange(2848, 3360)
    src[OFF_CG:OFF_CG + 512] = np.arange(3360, 3872)
    src[OFF_AAB:OFF_AAB + 32] = np.arange(2048, 2080)
    o16 = N_PROJ32
    src[o16 + OFF_BQ:o16 + OFF_BQ + 512] = np.arange(2080, 2592)
    src[o16 + OFF_DQ:o16 + OFF_DQ + 512] = np.arange(3872, 4384)
    src[o16 + OFF_DK:o16 + OFF_DK + 512] = np.arange(4384, 4896)
    src[o16 + OFF_DV:o16 + OFF_DV + 512] = np.arange(4896, 5408)
    for kv in range(2):
        for half in range(2):
            o = o16 + 128 * kv + 64 * half
            src[OFF_BK + o:OFF_BK + o + 64] = np.arange(2592 + 64 * kv, 2592 + 64 * kv + 64)
            src[OFF_BV + o:OFF_BV + o + 64] = np.arange(2720 + 64 * kv, 2720 + 64 * kv + 64)
    return src


def _cparams(sem, flags=None):
    return pltpu.CompilerParams(dimension_semantics=sem, vmem_limit_bytes=VMEM_LIMIT, flags=flags)


def _dot(a, b):
    return jnp.dot(a, b, preferred_element_type=f32)


def _dot_nt(a, b):
    return lax.dot_general(a, b, (((1,), (1,)), ((), ())), preferred_element_type=f32)


def _dot_tn(a, b):
    return lax.dot_general(a, b, (((0,), (0,)), ((), ())), preferred_element_type=f32)


def _split2(x):
    hi = x.astype(bf16)
    return hi, (x - hi.astype(f32)).astype(bf16)


def _dot_sel(x, sel):
    hi, lo = _split2(x)
    return _dot(hi, sel) + _dot(lo, sel)


def _sel_dot(sel, x):
    hi, lo = _split2(x)
    return _dot(sel, hi) + _dot(sel, lo)


def _sigmoid(x):
    return 1.0 / (1.0 + jnp.exp(-x))


def _softplus(x):
    return jnp.maximum(x, 0.0) + jnp.log(1.0 + jnp.exp(-jnp.abs(x)))


def _silu(x):
    return x * _sigmoid(x)


def _gelu_tanh(x):
    return 0.5 * x * (1.0 + jnp.tanh(math.sqrt(2.0 / math.pi) * (x + 0.044715 * (x * x * x))))


def _norm_mod(x, g, sc, sh):
    var = jnp.mean(x * x, axis=-1, keepdims=True)
    return (x * lax.rsqrt(var + EPS) * g) * (1.0 + sc) + sh


def _conv4(x, prev, nxt, w):
    T = x.shape[0]
    row = lax.broadcasted_iota(jnp.int32, x.shape, 0)
    xm2 = pltpu.roll(x, 2, 0)
    xm2 = jnp.where(row == 0, prev[6:7, :], jnp.where(row == 1, prev[7:8, :], xm2))
    xm1 = jnp.where(row == 0, prev[7:8, :], pltpu.roll(x, 1, 0))
    xp1 = jnp.where(row == T - 1, nxt[0:1, :], pltpu.roll(x, T - 1, 0))
    return w[0:1, :] * xm2 + w[1:2, :] * xm1 + w[2:3, :] * x + w[3:4, :] * xp1


def _mod_kernel(c_ref, w_ref, b_ref, o_ref):
    c = c_ref[...]
    o_ref[0] = _dot(_silu(c).astype(bf16), w_ref[0].astype(bf16)) + b_ref[0]


def _modulation(c_all, w_mod, b_mod):
    depth = w_mod.shape[0]
    tn = 1024
    return pl.pallas_call(
        _mod_kernel,
        grid=(depth, N_MOD // tn),
        in_specs=[
            pl.BlockSpec((MOD_ROWS, D_MODEL), lambda l, j: (0, 0)),
            pl.BlockSpec((1, D_MODEL, tn), lambda l, j: (l, 0, j)),
            pl.BlockSpec((1, 1, tn), lambda l, j: (l, 0, j)),
        ],
        out_specs=pl.BlockSpec((1, MOD_ROWS, tn), lambda l, j: (l, 0, j)),
        out_shape=jax.ShapeDtypeStruct((depth, MOD_ROWS, N_MOD), f32),
        compiler_params=_cparams(("arbitrary", "arbitrary")),
        name="modulation",
    )(c_all, w_mod, b_mod.reshape(depth, 1, N_MOD))


def _mod_spec(row0, chunk, ngrid):
    def imap(*a):
        b, l_ref = a[0], a[ngrid]
        return (l_ref[0] * (MOD_ROWS * 6) + (row0 + b) * 6 + chunk, 0, 0)
    return pl.BlockSpec((None, 1, D_MODEL), imap)


N_TILES32 = N_PROJ32 // PROJ_TN


def _norm_proj_kernel(l_ref, x_ref, g_ref, sc_ref, sh_ref, w_ref, o32_ref, o16_ref, h_ref):
    j = pl.program_id(2)

    @pl.when(j == 0)
    def _():
        h_ref[...] = _norm_mod(x_ref[...], g_ref[...], sc_ref[...], sh_ref[...]).astype(bf16)

    r = _dot(h_ref[...], w_ref[...])

    @pl.when(j < N_TILES32)
    def _():
        o32_ref[...] = r

    @pl.when(j >= N_TILES32)
    def _():
        o16_ref[...] = r.astype(bf16)


def _norm_proj(l, x, norm_g, mod, row0, w_p):
    B, S, _ = x.shape
    tm = min(1024, S)
    grid_spec = pltpu.PrefetchScalarGridSpec(
        num_scalar_prefetch=1,
        grid=(B, S // tm, N_PROJ // PROJ_TN),
        in_specs=[
            pl.BlockSpec((None, tm, D_MODEL), lambda b, i, j, l_ref: (b, i, 0)),
            pl.BlockSpec((None, 1, D_MODEL), lambda b, i, j, l_ref: (l_ref[0], 0, 0)),
            _mod_spec(row0, 1, 3),
            _mod_spec(row0, 0, 3),
            pl.BlockSpec((None, D_MODEL, PROJ_TN), lambda b, i, j, l_ref: (l_ref[0], 0, j)),
        ],
        out_specs=[
            pl.BlockSpec((None, tm, PROJ_TN), lambda b, i, j, l_ref: (b, i, jnp.minimum(j, N_TILES32 - 1))),
            pl.BlockSpec((None, tm, PROJ_TN), lambda b, i, j, l_ref: (b, i, jnp.maximum(j - N_TILES32, 0))),
        ],
        scratch_shapes=[pltpu.VMEM((tm, D_MODEL), bf16)],
    )
    return pl.pallas_call(
        _norm_proj_kernel,
        grid_spec=grid_spec,
        out_shape=[jax.ShapeDtypeStruct((B, S, N_PROJ32), f32), jax.ShapeDtypeStruct((B, S, N_PROJ16), bf16)],
        compiler_params=_cparams(("arbitrary", "arbitrary", "arbitrary")),
        name="norm_proj",
    )(l, x, norm_g, mod, mod, w_p)


def _gdn_kernel(l_ref, x_ref, xp_ref, xn_ref, ab_ref, *rest, rev, final):
    if final:
        (z_ref, ob_ref, cw_ref, alog_ref, dtb_ref, ng_ref, ehead_ref, expg_ref, expb_ref,
         o_ref, s_ref) = rest
    else:
        (cw_ref, alog_ref, dtb_ref, ehead_ref, expg_ref, expb_ref, o_ref, s_ref) = rest
    C = GDN_CHUNK
    TB = x_ref.shape[0]
    NC = TB // C
    i = pl.program_id(1)
    nb = pl.num_programs(1)
    blk = nb - 1 - i if rev else i

    @pl.when(i == 0)
    def _():
        s_ref[...] = jnp.zeros_like(s_ref)

    prev = jnp.where(blk > 0, xp_ref[...], 0.0)
    nxt = jnp.where(blk < nb - 1, xn_ref[...], 0.0)
    y = _silu(_conv4(x_ref[...], prev, nxt, cw_ref[...]))
    q, k, v = y[:, 0:512], y[:, 512:1024], y[:, 1024:1536]
    ehead = ehead_ref[...]
    qn = q * lax.rsqrt(_dot((q * q).astype(bf16), ehead) + EPS) * (HEAD_DIM ** -0.5)
    kn = k * lax.rsqrt(_dot((k * k).astype(bf16), ehead) + EPS)

    gin = ab_ref[...]
    g = -jnp.exp(alog_ref[...]) * _softplus(gin + dtb_ref[...])
    rt = lax.broadcasted_iota(jnp.int32, (TB, TB), 0)
    ct = lax.broadcasted_iota(jnp.int32, (TB, TB), 1)
    same = (rt // C) == (ct // C)
    ones_bd = jnp.where(same, 1.0, 0.0).astype(bf16)
    cum_bd = jnp.where(same & ((ct >= rt) if rev else (ct <= rt)), 1.0, 0.0).astype(bf16)
    gc = _sel_dot(cum_bd, g)
    gl = _sel_dot(ones_bd, g)
    expg = expg_ref[...]
    gc_x = _dot_sel(gc, expg)
    gl_x = _dot_sel(gl, expg)
    beta_x = _sigmoid(_dot_sel(gin, expb_ref[...]))
    egc_x = jnp.exp(gc_x)
    etail_x = jnp.exp(gl_x - gc_x)
    egl_x = jnp.exp(gl_x)

    kb = kn * beta_x
    vb = v * beta_x
    kw = kb * egc_x
    qh = qn * egc_x
    kt = kn * etail_x

    W = 4 * C
    r2 = lax.broadcasted_iota(jnp.int32, (W, W), 0) // C
    c2 = lax.broadcasted_iota(jnp.int32, (W, W), 1) // C
    bd = r2 == c2
    rs = lax.broadcasted_iota(jnp.int32, (C, W), 0)
    js = lax.broadcasted_iota(jnp.int32, (C, W), 1) % C
    tri = (js >= rs) if rev else (js <= rs)
    strict = (js > rs) if rev else (js < rs)
    eye_s = jnp.where(js == rs, 1.0, 0.0)
    ones_cc = jnp.ones((C, C), bf16)

    def blockdiag(m):
        mb = m.astype(bf16)
        return jnp.where(bd, jnp.concatenate([mb, mb, mb, mb], axis=0), jnp.zeros((), bf16))

    def dot_heads(x, y):
        return _dot(x.astype(bf16), blockdiag(y))

    pairs = [(c, grp) for c in range(NC) for grp in range(2)]

    def part(x, pr):
        c, grp = pr
        return x[C * c:C * c + C, W * grp:W * grp + W]

    a_mat, intra_w, t_inv = {}, {}, {}
    for pr in pairs:
        kst = blockdiag(part(kn, pr))
        kk = _dot_nt(part(kb, pr).astype(bf16), kst)
        qk = _dot_nt(part(qn, pr).astype(bf16), kst)
        col = part(gc_x, pr)
        rowv = _sel_dot(ones_cc, col * eye_s)
        dm = jnp.exp(jnp.where(tri, col - rowv, NEG))
        a_mat[pr] = jnp.where(strict, kk * dm, 0.0)
        intra_w[pr] = (qk * dm).astype(bf16)
        t_inv[pr] = eye_s - jnp.where(rs // 2 == js // 2, a_mat[pr], 0.0)
    s = 2
    while s < C:
        off = (rs // (2 * s) == js // (2 * s)) & (rs // s != js // s)
        ys = {pr: dot_heads(t_inv[pr], jnp.where(off, a_mat[pr], 0.0)) for pr in pairs}
        t_inv = {pr: t_inv[pr] - dot_heads(ys[pr], t_inv[pr]) for pr in pairs}
        s *= 2
    pre = {}
    for pr in pairs:
        tb = t_inv[pr].astype(bf16)
        pre[pr] = (_dot(tb, blockdiag(part(vb, pr))), _dot(tb, blockdiag(part(kw, pr))), intra_w[pr])

    rows = [None] * NC
    order = range(NC - 1, -1, -1) if rev else range(NC)
    sts = [s_ref[0], s_ref[1]]
    for c in order:
        rw = slice(C * c, C * c + C)
        outs = []
        for grp in range(2):
            sl = slice(W * grp, W * grp + W)
            u, w, intra = pre[(c, grp)]
            st = sts[grp]
            wq = jnp.concatenate([w, qh[rw, sl]], axis=0).astype(bf16)
            ws = _dot(wq, st.astype(bf16))
            vnew = u - ws[0:C]
            outs.append(ws[C:2 * C] + _dot(intra, blockdiag(vnew)))
            upd = _dot_tn(kt[rw, sl].astype(bf16), vnew.astype(bf16))
            sts[grp] = st * egl_x[C * c:C * c + 1, sl] + jnp.where(bd, upd, 0.0)
        rows[c] = jnp.concatenate(outs, axis=1)
    s_ref[0] = sts[0]
    s_ref[1] = sts[1]
    o_full = jnp.concatenate(rows, axis=0)
    if final:
        tot = o_full + ob_ref[...]
        ms = _dot((tot * tot).astype(bf16), ehead) * (1.0 / HEAD_DIM)
        o_ref[...] = (tot * lax.rsqrt(ms + EPS) * ng_ref[...] * _silu(z_ref[...])).astype(o_ref.dtype)
    else:
        o_ref[...] = o_full


def _gdn_consts():
    lane = np.arange(512)
    ehead = (lane[:, None] // HEAD_DIM == lane[None, :] // HEAD_DIM).astype(np.float32)
    expg = np.zeros((2, 128, 512), np.float32)
    expb = np.zeros((2, 128, 512), np.float32)
    for r in range(2):
        expg[r, 8 * r + lane // HEAD_DIM, lane] = 1.0
        expb[r, 16 + 8 * r + lane // HEAD_DIM, lane] = 1.0
    return jnp.asarray(ehead, bf16), jnp.asarray(expg, bf16), jnp.asarray(expb, bf16)


def _gdn_pass(l, proj, conv_w, alog_row, dtb_row, ng_row, consts, rev, ob):
    B, S, _ = proj.shape
    TB = min(GDN_BLOCK_CHUNKS * GDN_CHUNK, S)
    nb = S // TB
    n8 = S // 8
    final = not rev
    ehead, expg, expb = consts

    def pos(i):
        return nb - 1 - i if rev else i

    in_specs = [
        pl.BlockSpec((None, TB, 1536), lambda b, i, l_ref: (b, pos(i), 0)),
        pl.BlockSpec((None, 8, 1536), lambda b, i, l_ref: (b, jnp.maximum(pos(i) * (TB // 8) - 1, 0), 0)),
        pl.BlockSpec((None, 8, 1536), lambda b, i, l_ref: (b, jnp.minimum((pos(i) + 1) * (TB // 8), n8 - 1), 0)),
        pl.BlockSpec((None, TB, 128), lambda b, i, l_ref: (b, pos(i), OFF_AAB // 128)),
    ]
    args = [proj, proj, proj, proj]
    if final:
        in_specs += [
            pl.BlockSpec((None, TB, 512), lambda b, i, l_ref: (b, pos(i), OFF_AZ // 512)),
            pl.BlockSpec((None, TB, 512), lambda b, i, l_ref: (b, pos(i), 0)),
        ]
        args += [proj, ob]
    in_specs += [
        pl.BlockSpec((None, 4, 1536), lambda b, i, l_ref: (l_ref[0], 0, 0)),
        pl.BlockSpec((None, 1, 128), lambda b, i, l_ref: (l_ref[0], 0, 0)),
        pl.BlockSpec((None, 1, 128), lambda b, i, l_ref: (l_ref[0], 0, 0)),
    ]
    args += [conv_w, alog_row, dtb_row]
    if final:
        in_specs.append(pl.BlockSpec((None, 1, 512), lambda b, i, l_ref: (l_ref[0], 0, 0)))
        args.append(ng_row)
    in_specs += [
        pl.BlockSpec((512, 512), lambda b, i, l_ref: (0, 0)),
        pl.BlockSpec((None, 128, 512), lambda b, i, l_ref: (1 if rev else 0, 0, 0)),
        pl.BlockSpec((None, 128, 512), lambda b, i, l_ref: (1 if rev else 0, 0, 0)),
    ]
    args += [ehead, expg, expb]
    grid_spec = pltpu.PrefetchScalarGridSpec(
        num_scalar_prefetch=1,
        grid=(B, nb),
        in_specs=in_specs,
        out_specs=pl.BlockSpec((None, TB, 512), lambda b, i, l_ref: (b, pos(i), 0)),
        scratch_shapes=[pltpu.VMEM((2, 4 * GDN_CHUNK, 4 * GDN_CHUNK), f32)],
    )
    return pl.pallas_call(
        functools.partial(_gdn_kernel, rev=rev, final=final),
        grid_spec=grid_spec,
        out_shape=jax.ShapeDtypeStruct((B, S, 512), bf16 if final else f32),
        compiler_params=_cparams(("arbitrary", "arbitrary")),
        name="gdn_fwd" if final else "gdn_bwd",
    )(l, *args)


def _swa_kernel(l_ref, sink_ref, q_ref, kp_ref, kc_ref, kn_ref, vp_ref, vc_ref, vn_ref, o_ref):
    T = SWA_BLOCK
    nsub = q_ref.shape[0] // T
    i = pl.program_id(1)
    nb = pl.num_programs(1) * nsub
    l = l_ref[0]
    row = lax.broadcasted_iota(jnp.int32, (2 * T, 3 * T), 0)
    col = lax.broadcasted_iota(jnp.int32, (2 * T, 3 * T), 1)
    top = row < T
    r = jnp.where(top, row, row - T)
    disti = jnp.abs(r + T - col)
    dist = disti.astype(f32)
    lo = lax.broadcasted_iota(jnp.int32, (T, 128), 1) < HEAD_DIM
    top1 = lax.broadcasted_iota(jnp.int32, (2 * T, 1), 0) < T
    for kv in range(2):
        ksl = slice(128 * kv, 128 * kv + 128)
        kall = jnp.concatenate([kp_ref[:, ksl], kc_ref[:, ksl], kn_ref[:, ksl]], axis=0)
        vall = jnp.concatenate([vp_ref[:, ksl], vc_ref[:, ksl], vn_ref[:, ksl]], axis=0)
        for j in range(nsub):
            blk = i * nsub + j
            valid = (disti <= SWA_BLOCK) & ((col >= T) | (blk > 0)) & ((col < 2 * T) | (blk < nb - 1))
            kd = kall[j * T:(j + 3) * T]
            vd = vall[j * T:(j + 3) * T]
            for pp in range(2):
                p = 2 * kv + pp
                qp = q_ref[j * T:(j + 1) * T, 128 * p:128 * p + 128].astype(f32) * (HEAD_DIM ** -0.5)
                lhs = jnp.concatenate([jnp.where(lo, qp, 0.0), jnp.where(lo, 0.0, qp)], axis=0).astype(bf16)
                s = _dot_nt(lhs, kd)
                slope = jnp.where(top, 2.0 ** -(2 * p + 1), 2.0 ** -(2 * p + 2))
                s = jnp.where(valid, s - slope * dist, NEG)
                sk = jnp.where(top1, sink_ref[l, 2 * p], sink_ref[l, 2 * p + 1])
                m = jnp.maximum(jnp.max(s, axis=1, keepdims=True), sk)
                e = jnp.exp(s - m)
                den = jnp.sum(e, axis=1, keepdims=True) + jnp.exp(sk - m)
                o = _dot(e.astype(bf16), vd) / den
                o_ref[j * T:(j + 1) * T, 128 * p:128 * p + 128] = (
                    jnp.where(lo, o[0:T], o[T:2 * T]).astype(o_ref.dtype))


def _swa(l, proj16, sink):
    B, S, _ = proj16.shape
    T = SWA_BLOCK
    nsub = min(4, S // T)
    TB = nsub * T
    nb = S // T
    kb, vb = OFF_BK // 256, OFF_BV // 256

    def side(blk, which):
        if which < 0:
            return lambda b, i, l_ref: (b, jnp.maximum(i * nsub - 1, 0), blk)
        return lambda b, i, l_ref: (b, jnp.minimum((i + 1) * nsub, nb - 1), blk)

    grid_spec = pltpu.PrefetchScalarGridSpec(
        num_scalar_prefetch=1,
        grid=(B, S // TB),
        in_specs=[
            pl.BlockSpec(memory_space=pltpu.SMEM),
            pl.BlockSpec((None, TB, 512), lambda b, i, l_ref: (b, i, OFF_BQ // 512)),
            pl.BlockSpec((None, T, 256), side(kb, -1)),
            pl.BlockSpec((None, TB, 256), lambda b, i, l_ref: (b, i, kb)),
            pl.BlockSpec((None, T, 256), side(kb, 1)),
            pl.BlockSpec((None, T, 256), side(vb, -1)),
            pl.BlockSpec((None, TB, 256), lambda b, i, l_ref: (b, i, vb)),
            pl.BlockSpec((None, T, 256), side(vb, 1)),
        ],
        out_specs=pl.BlockSpec((None, TB, 512), lambda b, i, l_ref: (b, i, 0)),
    )
    return pl.pallas_call(
        _swa_kernel,
        grid_spec=grid_spec,
        out_shape=jax.ShapeDtypeStruct((B, S, 512), bf16),
        compiler_params=_cparams(("arbitrary", "arbitrary")),
        name="swa",
    )(l, sink, proj16, proj16, proj16, proj16, proj16, proj16, proj16)


def _lru_kernel(l_ref, x_ref, xp_ref, xn_ref, *rest, rev, final):
    if final:
        (gate_ref, hb_ref, cw_ref, cb_ref, wg_ref, bg_ref, lam_ref, o_ref, carry_ref) = rest
    else:
        (cw_ref, cb_ref, wg_ref, bg_ref, lam_ref, o_ref, carry_ref) = rest
    T = x_ref.shape[0]
    i = pl.program_id(1)
    nb = pl.num_programs(1)
    blk = nb - 1 - i if rev else i

    @pl.when(i == 0)
    def _():
        carry_ref[...] = jnp.zeros_like(carry_ref)

    prev = jnp.where(blk > 0, xp_ref[...], 0.0)
    nxt = jnp.where(blk < nb - 1, xn_ref[...], 0.0)
    xc = _conv4(x_ref[...], prev, nxt, cw_ref[...]) + cb_ref[...]
    gates = _sigmoid(_dot(xc.astype(bf16), wg_ref[...]) + bg_ref[...])
    rg, ig = gates[:, 0:512], gates[:, 512:1024]
    log_a = (-LRU_C) * rg * _softplus(-lam_ref[...])
    a = jnp.exp(log_a)
    y2 = 2.0 * log_a
    one_minus_a2 = -jnp.tanh(0.5 * y2) * (jnp.exp(y2) + 1.0)
    bv = jnp.sqrt(one_minus_a2) * ig * xc
    row = lax.broadcasted_iota(jnp.int32, (T, 512), 0)
    s = 1
    while s < T:
        if rev:
            ok = row < T - s
            a_s, b_s = pltpu.roll(a, T - s, 0), pltpu.roll(bv, T - s, 0)
        else:
            ok = row >= s
            a_s, b_s = pltpu.roll(a, s, 0), pltpu.roll(bv, s, 0)
        bv = jnp.where(ok, a * b_s + bv, bv)
        a = jnp.where(ok, a * a_s, a)
        s *= 2
    h = a * carry_ref[...] + bv
    carry_ref[...] = h[0:1, :] if rev else h[T - 1:T, :]
    if final:
        o_ref[...] = ((h + hb_ref[...]) * _gelu_tanh(gate_ref[...])).astype(o_ref.dtype)
    else:
        o_ref[...] = h


def _lru_pass(l, proj, conv_w, conv_b, wg, bg, lam, rev, hb):
    B, S, _ = proj.shape
    T = min(256, S)
    nb = S // T
    n8 = S // 8
    final = not rev
    d = 1 if rev else 0
    xb = OFF_CX // 512

    def pos(i):
        return nb - 1 - i if rev else i

    in_specs = [
        pl.BlockSpec((None, T, 512), lambda b, i, l_ref: (b, pos(i), xb)),
        pl.BlockSpec((None, 8, 512), lambda b, i, l_ref: (b, jnp.maximum(pos(i) * (T // 8) - 1, 0), xb)),
        pl.BlockSpec((None, 8, 512), lambda b, i, l_ref: (b, jnp.minimum((pos(i) + 1) * (T // 8), n8 - 1), xb)),
    ]
    args = [proj, proj, proj]
    if final:
        in_specs += [
            pl.BlockSpec((None, T, 512), lambda b, i, l_ref: (b, pos(i), OFF_CG // 512)),
            pl.BlockSpec((None, T, 512), lambda b, i, l_ref: (b, pos(i), 0)),
        ]
        args += [proj, hb]
    in_specs += [
        pl.BlockSpec((None, 4, 512), lambda b, i, l_ref: (l_ref[0], 0, 0)),
        pl.BlockSpec((None, 1, 512), lambda b, i, l_ref: (l_ref[0], 0, 0)),
        pl.BlockSpec((None, None, 512, 1024), lambda b, i, l_ref: (l_ref[0], d, 0, 0)),
        pl.BlockSpec((None, None, 1, 1024), lambda b, i, l_ref: (l_ref[0], d, 0, 0)),
        pl.BlockSpec((None, None, 1, 512), lambda b, i, l_ref: (l_ref[0], d, 0, 0)),
    ]
    args += [conv_w, conv_b, wg, bg, lam]
    grid_spec = pltpu.PrefetchScalarGridSpec(
        num_scalar_prefetch=1,
        grid=(B, nb),
        in_specs=in_specs,
        out_specs=pl.BlockSpec((None, T, 512), lambda b, i, l_ref: (b, pos(i), 0)),
        scratch_shapes=[pltpu.VMEM((1, 512), f32)],
    )
    return pl.pallas_call(
        functools.partial(_lru_kernel, rev=rev, final=final),
        grid_spec=grid_spec,
        out_shape=jax.ShapeDtypeStruct((B, S, 512), bf16 if final else f32),
        compiler_params=_cparams(("arbitrary", "arbitrary")),
        name="lru_fwd" if final else "lru_bwd",
    )(l, *args)


def _diff_feat_consts(tk):
    c = np.arange(tk)
    feat = np.zeros((tk, 128), np.float32)
    feat[:, 0] = 1.0
    feat[:, 1] = c % 256
    feat[:, 2] = c // 256
    return jnp.asarray(feat, bf16)


def _diff_kernel(l_ref, laminit_ref, q_ref, k_ref, vt_ref, feat_ref, lv_ref, g_ref, o_ref,
                 lhs_ref, m_ref, alpha_ref, acc_ref, sbuf0_ref, sbuf1_ref, pbuf0_ref, pbuf1_ref, kn2_ref):
    sbufs = (sbuf0_ref, sbuf1_ref)
    pbufs = (pbuf0_ref, pbuf1_ref)
    tq = q_ref.shape[0]
    nk, _, tk = vt_ref.shape
    assert nk % 2 == 0
    RB = 64
    h = pl.program_id(1)
    qi = pl.program_id(2)
    q0 = qi * tq
    slope = jnp.where(h == 0, 2.0 ** -2, jnp.where(h == 1, 2.0 ** -4, jnp.where(h == 2, 2.0 ** -6, 2.0 ** -8)))

    qp = q_ref[...].astype(f32) * (HEAD_DIM ** -0.5)
    lane = lax.broadcasted_iota(jnp.int32, (2 * tq, 128), 1)
    rr = lax.broadcasted_iota(jnp.int32, (2 * tq, 128), 0)
    rr = jnp.where(rr < tq, rr, rr - tq).astype(f32)
    lo = lax.broadcasted_iota(jnp.int32, (tq, 128), 1) < HEAD_DIM
    qm = jnp.concatenate([jnp.where(lo, qp, 0.0), jnp.where(lo, 0.0, qp)], axis=0).astype(bf16)
    fplus = jnp.where(lane == 0, -slope * rr,
                      jnp.where(lane == 1, slope, jnp.where(lane == 2, 256.0 * slope, 0.0)))
    lhs_ref[0] = jnp.concatenate([qm, fplus.astype(bf16)], axis=1)
    lhs_ref[1] = jnp.concatenate([qm, (-fplus).astype(bf16)], axis=1)
    lhs_ref[2] = jnp.concatenate([qm, jnp.zeros((2 * tq, 128), bf16)], axis=1)
    m_ref[...] = jnp.full_like(m_ref, NEG)
    alpha_ref[...] = jnp.ones_like(alpha_ref)
    acc_ref[...] = jnp.zeros_like(acc_ref)
    pbuf1_ref[...] = jnp.zeros_like(pbuf1_ref)

    def variant(ki):
        d = q0 - ki * tk
        return d, jnp.where(d >= tk, 0, jnp.where(d <= -tq, 1, 2))

    def scores(ki, slot):
        ki = jnp.minimum(ki, nk - 1)
        _, var = variant(ki)
        kx = jnp.concatenate([k_ref[pl.ds(pl.multiple_of(ki * tk, tk), tk), :], feat_ref[...]], axis=1)
        sbufs[slot][...] = _dot_nt(kx, lhs_ref[var])

    def accumulate(ki, slot):
        acc_ref[...] = alpha_ref[slot] * acc_ref[...] + _dot(vt_ref[ki], pbufs[slot][...])

    def softmax(ki, slot):
        d, var = variant(ki)
        mx = jnp.full((8, 2 * tq), NEG, f32)
        for rb in range(tk // RB):
            sb = sbufs[slot][rb * RB:rb * RB + RB, :]
            for r8 in range(RB // 8):
                mx = jnp.maximum(mx, sb[8 * r8:8 * r8 + 8, :])
        dabs = jnp.abs(jnp.zeros((1, 2 * tq), jnp.int32) + d).astype(f32)
        cd = jnp.where(var == 2, 0.0, slope * dabs)
        m_prev = m_ref[...]
        m_new = jnp.maximum(m_prev, jnp.max(mx, axis=0, keepdims=True) - cd)
        alpha_ref[slot] = jnp.exp(m_prev - m_new)
        m_ref[...] = m_new
        shift = m_new + cd
        for rb in range(tk // RB):
            rows = slice(rb * RB, rb * RB + RB)
            pbufs[slot][rows, :] = jnp.exp(sbufs[slot][rows, :] - shift).astype(bf16)

    def diag_bias(ki, slot):
        d, var = variant(ki)

        @pl.when(var == 2)
        def _():
            for rb in range(tk // RB):
                rows = slice(rb * RB, rb * RB + RB)
                ci = lax.broadcasted_iota(jnp.int32, (RB, 2 * tq), 0) + (rb * RB)
                ji = lax.broadcasted_iota(jnp.int32, (RB, 2 * tq), 1)
                r = jnp.where(ji < tq, ji, ji - tq)
                sbufs[slot][rows, :] = sbufs[slot][rows, :] - slope * jnp.abs(r - ci + d).astype(f32)

    @pl.when(qi == 0)
    def _():
        def knorm(i, mx):
            kk = k_ref[pl.ds(pl.multiple_of(i * tq, tq), tq), :].astype(f32)
            sq = _dot((kk * kk).astype(bf16), jnp.ones((128, 128), bf16))
            for r8 in range(tq // 8):
                mx = jnp.maximum(mx, sq[8 * r8:8 * r8 + 8, :])
            return mx
        mx = lax.fori_loop(0, (nk * tk) // tq, knorm, jnp.zeros((8, 128), f32))
        kn2_ref[0] = jnp.max(mx)

    qf = q_ref[...].astype(f32)
    qn2 = jnp.max(_dot((qf * qf).astype(bf16), jnp.ones((128, 128), bf16)), keepdims=True)
    krows = k_ref[pl.ds(pl.multiple_of(q0, tq), tq), :].astype(f32)
    lane_m = lax.broadcasted_iota(jnp.int32, (128, 128), 0) // HEAD_DIM
    lane_n = lax.broadcasted_iota(jnp.int32, (128, 128), 1) // HEAD_DIM
    same_map = jnp.where(lane_m == lane_n, 1.0, 0.0).astype(bf16)
    self_t = _dot((qf * krows).astype(bf16), same_map) * (HEAD_DIM ** -0.5)
    m_low = jnp.min(self_t, keepdims=True)
    qk_hi = jnp.sqrt(qn2 * kn2_ref[0]) * (HEAD_DIM ** -0.5)
    dist = (1.02 * qk_hi + 1.0 - m_low + 105.0) / slope
    q0f = (jnp.zeros((1, 1), jnp.int32) + q0).astype(f32)
    first = jnp.floor((q0f - (tk - 1) - dist) / tk) + 1.0
    last = jnp.ceil((dist + q0f + (tq - 1)) / tk)
    first = jnp.max(jnp.clip(first, 0.0, float(nk)).astype(jnp.int32))
    last = jnp.max(jnp.clip(last, 0.0, float(nk)).astype(jnp.int32))
    p_first = first // 2
    p_last = (last + 1) // 2

    scores(2 * p_first, 0)

    def body(j, carry):
        ka = 2 * j
        diag_bias(ka, 0)
        softmax(ka, 0)
        scores(ka + 1, 1)
        accumulate(jnp.maximum(ka - 1, 0), 1)
        diag_bias(ka + 1, 1)
        softmax(ka + 1, 1)
        scores(ka + 2, 0)
        accumulate(ka, 0)
        return carry

    lax.fori_loop(p_first, p_last, body, 0)
    accumulate(2 * p_last - 1, 1)

    lam_init = laminit_ref[l_ref[0]]
    lv = lv_ref[...]
    lam = (jnp.exp(jnp.sum(lv[0:1] * lv[1:2], axis=1, keepdims=True))
           - jnp.exp(jnp.sum(lv[2:3] * lv[3:4], axis=1, keepdims=True)) + lam_init)
    acc = acc_ref[...]
    ot = acc[0:128, :] / acc[128:129, :]
    od = (ot[:, 0:tq] - lam * ot[:, tq:2 * tq]).T
    var = jnp.mean(od * od, axis=-1, keepdims=True)
    o_ref[...] = (od * lax.rsqrt(var + EPS) * g_ref[...] * (1.0 - lam_init)).astype(o_ref.dtype)


def _diff(l, proj16, lam_init_tab, diff_lambda, norm_g):
    B, S, _ = proj16.shape
    tq = min(DIFF_TQ, S)
    tk = min(DIFF_TK, S // 4)
    nk = S // tk
    vt = proj16[:, :, OFF_DV:OFF_DV + 512].reshape(B, nk, tk, DIFF_HEADS, 128).transpose(0, 3, 1, 4, 2)
    extra = jnp.zeros((B, DIFF_HEADS, nk, DIFF_VROWS - 128, tk), bf16).at[:, :, :, 0, :].set(1.0)
    vt = jnp.concatenate([vt, extra], axis=3)
    grid_spec = pltpu.PrefetchScalarGridSpec(
        num_scalar_prefetch=1,
        grid=(B, DIFF_HEADS, S // tq),
        in_specs=[
            pl.BlockSpec(memory_space=pltpu.SMEM),
            pl.BlockSpec((None, tq, 128), lambda b, h, qi, l_ref: (b, qi, OFF_DQ // 128 + h)),
            pl.BlockSpec((None, S, 128), lambda b, h, qi, l_ref: (b, 0, OFF_DK // 128 + h)),
            pl.BlockSpec((None, None, nk, DIFF_VROWS, tk), lambda b, h, qi, l_ref: (b, h, 0, 0, 0)),
            pl.BlockSpec((tk, 128), lambda b, h, qi, l_ref: (0, 0)),
            pl.BlockSpec((None, 4, HEAD_DIM), lambda b, h, qi, l_ref: (l_ref[0], 0, 0)),
            pl.BlockSpec((None, 1, 128), lambda b, h, qi, l_ref: (l_ref[0], 0, 0)),
        ],
        out_specs=pl.BlockSpec((None, tq, 128), lambda b, h, qi, l_ref: (b, qi, h)),
        scratch_shapes=[
            pltpu.VMEM((3, 2 * tq, 256), bf16),
            pltpu.VMEM((1, 2 * tq), f32),
            pltpu.VMEM((2, 1, 2 * tq), f32),
            pltpu.VMEM((DIFF_VROWS, 2 * tq), f32),
            pltpu.VMEM((tk, 2 * tq), f32),
            pltpu.VMEM((tk, 2 * tq), f32),
            pltpu.VMEM((tk, 2 * tq), bf16),
            pltpu.VMEM((tk, 2 * tq), bf16),
            pltpu.SMEM((1,), f32),
        ],
    )
    return pl.pallas_call(
        _diff_kernel,
        grid_spec=grid_spec,
        out_shape=jax.ShapeDtypeStruct((B, S, 512), bf16),
        compiler_params=_cparams(("arbitrary", "arbitrary", "arbitrary")),
        name="diff_attn",
    )(l, lam_init_tab, proj16, proj16, vt, _diff_feat_consts(tk), diff_lambda, norm_g)


def _outproj_kernel(l_ref, x_ref, oa_ref, ob_ref, oc_ref, od_ref, gate_ref, w_ref, o_ref):
    acc = _dot(oa_ref[...], w_ref[0:512, :])
    acc += _dot(ob_ref[...], w_ref[512:1024, :])
    acc += _dot(oc_ref[...], w_ref[1024:1536, :])
    acc += _dot(od_ref[...], w_ref[1536:2048, :])
    o_ref[...] = x_ref[...] + gate_ref[...] * acc


def _outproj(l, x, oa, ob, oc, od, mod, row0, w_out):
    B, S, _ = x.shape
    tm = min(512, S)
    mix = pl.BlockSpec((None, tm, 512), lambda b, i, l_ref: (b, i, 0))
    grid_spec = pltpu.PrefetchScalarGridSpec(
        num_scalar_prefetch=1,
        grid=(B, S // tm),
        in_specs=[
            pl.BlockSpec((None, tm, D_MODEL), lambda b, i, l_ref: (b, i, 0)),
            mix, mix, mix, mix,
            _mod_spec(row0, 2, 2),
            pl.BlockSpec((None, D_MODEL, D_MODEL), lambda b, i, l_ref: (l_ref[0], 0, 0)),
        ],
        out_specs=pl.BlockSpec((None, tm, D_MODEL), lambda b, i, l_ref: (b, i, 0)),
    )
    return pl.pallas_call(
        _outproj_kernel,
        grid_spec=grid_spec,
        out_shape=jax.ShapeDtypeStruct((B, S, D_MODEL), f32),
        compiler_params=_cparams(("arbitrary", "arbitrary")),
        name="out_proj",
    )(l, x, oa, ob, oc, od, mod, w_out)


def _mlp_kernel(l_ref, x_ref, g_ref, sc_ref, sh_ref, gate_ref, w1_ref, w2_ref, o_ref, h_ref):
    f = pl.program_id(2)

    @pl.when(f == 0)
    def _():
        h_ref[...] = _norm_mod(x_ref[...], g_ref[...], sc_ref[...], sh_ref[...]).astype(bf16)
        o_ref[...] = jnp.zeros_like(o_ref)

    t = jnp.maximum(_dot(h_ref[...], w1_ref[...]), 0.0)
    o_ref[...] += _dot((t * t).astype(bf16), w2_ref[...])

    @pl.when(f == pl.num_programs(2) - 1)
    def _():
        o_ref[...] = x_ref[...] + gate_ref[...] * o_ref[...]


def _mlp(l, x, norm_g, mod, row0, w1, w2):
    B, S, _ = x.shape
    tm = min(1024, S)
    tf = 512
    grid_spec = pltpu.PrefetchScalarGridSpec(
        num_scalar_prefetch=1,
        grid=(B, S // tm, D_FF // tf),
        in_specs=[
            pl.BlockSpec((None, tm, D_MODEL), lambda b, i, f, l_ref: (b, i, 0)),
            pl.BlockSpec((None, 1, D_MODEL), lambda b, i, f, l_ref: (l_ref[0], 0, 0)),
            _mod_spec(row0, 4, 3),
            _mod_spec(row0, 3, 3),
            _mod_spec(row0, 5, 3),
            pl.BlockSpec((None, D_MODEL, tf), lambda b, i, f, l_ref: (l_ref[0], 0, f)),
            pl.BlockSpec((None, tf, D_MODEL), lambda b, i, f, l_ref: (l_ref[0], f, 0)),
        ],
        out_specs=pl.BlockSpec((None, tm, D_MODEL), lambda b, i, f, l_ref: (b, i, 0)),
        scratch_shapes=[pltpu.VMEM((tm, D_MODEL), bf16)],
    )
    return pl.pallas_call(
        _mlp_kernel,
        grid_spec=grid_spec,
        out_shape=jax.ShapeDtypeStruct((B, S, D_MODEL), f32),
        compiler_params=_cparams(("arbitrary", "arbitrary", "arbitrary")),
        name="mlp",
    )(l, x, norm_g, mod, mod, mod, w1, w2)


def _final_norm_kernel(x_ref, g_ref, o_ref):
    x = x_ref[...]
    var = jnp.mean(x * x, axis=-1, keepdims=True)
    o_ref[...] = x * lax.rsqrt(var + EPS) * g_ref[...]


def _final_norm(x, g):
    B, S, _ = x.shape
    tm = min(1024, S)
    return pl.pallas_call(
        _final_norm_kernel,
        grid=(B, S // tm),
        in_specs=[
            pl.BlockSpec((None, tm, D_MODEL), lambda b, i: (b, i, 0)),
            pl.BlockSpec((1, D_MODEL), lambda b, i: (0, 0)),
        ],
        out_specs=pl.BlockSpec((None, tm, D_MODEL), lambda b, i: (b, i, 0)),
        out_shape=jax.ShapeDtypeStruct((B, S, D_MODEL), f32),
        compiler_params=_cparams(("arbitrary", "arbitrary")),
        name="final_norm",
    )(x, g.reshape(1, D_MODEL))


def _prepare(norm1_g, norm2_g, w_in, w_out, gdn_conv, gdn_a_log, gdn_dt_bias, gdn_norm_g, swa_sink,
             lru_conv, lru_conv_b, lru_w_gate, lru_b_gate, lru_lambda, diff_lambda, diff_norm_g,
             mlp_w1, mlp_w2):
    depth = w_in.shape[0]
    src = _proj_columns()
    w_p = jnp.where(jnp.asarray(src >= 0)[None, None, :],
                    jnp.take(w_in, jnp.asarray(np.maximum(src, 0)), axis=2), 0.0).astype(bf16)
    pad96 = jnp.zeros((depth, 1, 96), f32)
    alog_row = jnp.concatenate([gdn_a_log.reshape(depth, 1, 16), jnp.zeros((depth, 1, 16), f32), pad96], axis=2)
    dtb_row = jnp.concatenate([gdn_dt_bias.reshape(depth, 1, 16), jnp.zeros((depth, 1, 16), f32), pad96], axis=2)
    wg = jnp.einsum('lrgnde,nm->lrndgme', lru_w_gate, jnp.eye(8, dtype=f32)).reshape(depth, 2, 512, 1024)
    return dict(
        norm1=norm1_g.reshape(depth, 1, D_MODEL),
        norm2=norm2_g.reshape(depth, 1, D_MODEL),
        w_p=w_p,
        w_out=w_out.astype(bf16),
        gdn_conv=gdn_conv,
        alog_row=alog_row,
        dtb_row=dtb_row,
        gdn_ng=jnp.tile(gdn_norm_g, (1, GDN_HEADS)).reshape(depth, 1, 512),
        gdn_consts=_gdn_consts(),
        sink=swa_sink,
        lru_conv=lru_conv,
        lru_conv_b=lru_conv_b.reshape(depth, 1, 512),
        lru_wg=wg.astype(bf16),
        lru_bg=lru_b_gate.reshape(depth, 2, 1, 1024),
        lru_lam=lru_lambda.reshape(depth, 2, 1, 512),
        diff_lambda=diff_lambda,
        diff_ng=diff_norm_g.reshape(depth, 1, 128),
        lam_init=jnp.asarray([0.8 - 0.6 * math.exp(-0.3 * i) for i in range(depth)], f32),
        w1=mlp_w1.astype(bf16),
        w2=mlp_w2.astype(bf16),
    )


def _layer(l, x, mod, row0, p):
    proj, proj16 = _norm_proj(l, x, p['norm1'], mod, row0, p['w_p'])
    gdn_args = (l, proj, p['gdn_conv'], p['alog_row'], p['dtb_row'], p['gdn_ng'], p['gdn_consts'])
    o_rev = _gdn_pass(*gdn_args, True, None)
    o_a = _gdn_pass(*gdn_args, False, o_rev)
    o_b = _swa(l, proj16, p['sink'])
    lru_args = (l, proj, p['lru_conv'], p['lru_conv_b'], p['lru_wg'], p['lru_bg'], p['lru_lam'])
    h_rev = _lru_pass(*lru_args, True, None)
    o_c = _lru_pass(*lru_args, False, h_rev)
    o_d = _diff(l, proj16, p['lam_init'], p['diff_lambda'], p['diff_ng'])
    x = _outproj(l, x, o_a, o_b, o_c, o_d, mod, row0, p['w_out'])
    return _mlp(l, x, p['norm2'], mod, row0, p['w1'], p['w2'])


def kernel(x_prompt, x_sample, c_prompt, c_sample, norm1_g, norm2_g, w_mod, b_mod, w_in, w_out, gdn_conv, gdn_a_log, gdn_dt_bias, gdn_norm_g, swa_sink, lru_conv, lru_conv_b, lru_w_gate, lru_b_gate, lru_lambda, diff_lambda, diff_norm_g, mlp_w1, mlp_w2, final_g):
    depth = w_in.shape[0]
    bp, bs = x_prompt.shape[0], x_sample.shape[0]
    assert bp + bs <= MOD_ROWS
    c_all = jnp.concatenate([c_prompt, c_sample, jnp.zeros((MOD_ROWS - bp - bs, D_MODEL), f32)], axis=0)
    mod = _modulation(c_all, w_mod, b_mod).reshape(depth * MOD_ROWS * 6, 1, D_MODEL)
    p = _prepare(norm1_g, norm2_g, w_in, w_out, gdn_conv, gdn_a_log, gdn_dt_bias, gdn_norm_g, swa_sink,
                 lru_conv, lru_conv_b, lru_w_gate, lru_b_gate, lru_lambda, diff_lambda, diff_norm_g,
                 mlp_w1, mlp_w2)

    def body(i, xs):
        l = jnp.full((1,), i, jnp.int32)
        return (_layer(l, xs[0], mod, 0, p), _layer(l, xs[1], mod, bp, p))

    xp, xs = lax.fori_loop(0, depth, body, (x_prompt, x_sample))
    return (_final_norm(xp, final_g), _final_norm(xs, final_g))
```
